```python
import math
import jax, jax.numpy as jnp
from jax import lax
import numpy as np

D_MODEL = 1024
BATCH = 8
SEQ = 4096
DEPTH = 2

N_MIXERS = 2
N_SB = (DEPTH + 1) // 2
N_RET = DEPTH // 2
SB_HEADS = 16
SB_HEAD_DIM = D_MODEL // SB_HEADS
SB_BLOCK = 128
RET_HEADS = D_MODEL // 256
RET_KEY_DIM = D_MODEL // RET_HEADS
RET_VAL_DIM = 2 * D_MODEL // RET_HEADS
RET_CHUNK = 128
ROPE_BASE = 10000.0
D_FF = 4 * D_MODEL
PLE_DIM = 256
EPS = 1e-6

kernel_name = "hybrid_stickbreak_retention_trunk"


def rms_norm(x, g):
    xf = x.astype(jnp.float32)
    y = xf * lax.rsqrt(jnp.mean(xf * xf, axis=-1, keepdims=True) + EPS)
    return (y * g.astype(jnp.float32)).astype(x.dtype)


def stick_breaking_attention(h, w_in, w_out):
    B, S, _ = h.shape
    qkv = h @ w_in
    q, k, v = jnp.split(qkv, 3, axis=-1)
    q = q.reshape(B, S, SB_HEADS, SB_HEAD_DIM).astype(jnp.float32)
    k = k.reshape(B, S, SB_HEADS, SB_HEAD_DIM).astype(jnp.float32)
    v = v.reshape(B, S, SB_HEADS, SB_HEAD_DIM).astype(jnp.float32)
    scale = SB_HEAD_DIM ** -0.5
    outs = []
    for blk in range(S // SB_BLOCK):
        start = blk * SB_BLOCK
        end = start + SB_BLOCK
        qb = q[:, start:end]
        kb = k[:, :end]
        vb = v[:, :end]
        z = jnp.einsum('bthd,bshd->bhts', qb, kb) * scale
        t_pos = start + jnp.arange(SB_BLOCK)
        s_pos = jnp.arange(end)
        mask = s_pos[None, :] < t_pos[:, None]
        log_1m_beta = jnp.where(mask, jax.nn.log_sigmoid(-z), 0.0)
        suffix = lax.cumsum(log_1m_beta, axis=3, reverse=True) - log_1m_beta
        a = jnp.where(mask, jnp.exp(jax.nn.log_sigmoid(z) + suffix), 0.0)
        outs.append(jnp.einsum('bhts,bshd->bthd', a, vb))
    o = jnp.concatenate(outs, axis=1).reshape(B, S, D_MODEL).astype(h.dtype)
    return o @ w_out


def apply_rope(x, pos):
    d = x.shape[-1]
    inv_freq = ROPE_BASE ** (-jnp.arange(0, d, 2, dtype=jnp.float32) / d)
    ang = pos.astype(jnp.float32)[:, None] * inv_freq[None, :]
    cos = jnp.cos(ang)[None, :, None, :]
    sin = jnp.sin(ang)[None, :, None, :]
    x1, x2 = jnp.split(x, 2, axis=-1)
    return jnp.concatenate([x1 * cos - x2 * sin, x1 * sin + x2 * cos], axis=-1)


def retention(h, w_in, w_out):
    B, S, _ = h.shape
    proj = h @ w_in
    q, k, v, g = jnp.split(proj, [D_MODEL, 2 * D_MODEL, 4 * D_MODEL], axis=-1)
    pos = jnp.arange(S)
    q = apply_rope(q.reshape(B, S, RET_HEADS, RET_KEY_DIM).astype(jnp.float32), pos)
    k = apply_rope(k.reshape(B, S, RET_HEADS, RET_KEY_DIM).astype(jnp.float32), pos) * (RET_KEY_DIM ** -0.5)
    v = v.reshape(B, S, RET_HEADS, RET_VAL_DIM).astype(jnp.float32)

    n_chunks = S // RET_CHUNK
    def to_chunks(t):
        return t.reshape(B, n_chunks, RET_CHUNK, RET_HEADS, t.shape[-1]).transpose(1, 0, 3, 2, 4)
    qc, kc, vc = to_chunks(q), to_chunks(k), to_chunks(v)

    log_gamma = jnp.log1p(-jnp.exp2(-5.0 - jnp.arange(RET_HEADS, dtype=jnp.float32)))
    idx = jnp.arange(RET_CHUNK, dtype=jnp.float32)
    diff = idx[:, None] - idx[None, :]
    intra_decay = jnp.where(diff >= 0, jnp.exp(log_gamma[:, None, None] * jnp.maximum(diff, 0.0)), 0.0)
    cross_decay = jnp.exp(log_gamma[:, None] * (idx + 1.0))[None, :, :, None]
    kv_decay = jnp.exp(log_gamma[:, None] * (RET_CHUNK - 1.0 - idx))[None, :, :, None]
    chunk_decay = jnp.exp(log_gamma * RET_CHUNK)[None, :, None, None]

    def step(state, inp):
        qb, kb, vb = inp
        scores = jnp.einsum('bhnd,bhmd->bhnm', qb, kb) * intra_decay[None]
        inner = jnp.einsum('bhnm,bhme->bhne', scores, vb)
        cross = jnp.einsum('bhnd,bhde->bhne', qb, state) * cross_decay
        state = state * chunk_decay + jnp.einsum('bhmd,bhme->bhde', kb * kv_decay, vb)
        return state, inner + cross

    state0 = jnp.zeros((B, RET_HEADS, RET_KEY_DIM, RET_VAL_DIM), jnp.float32)
    _, ys = lax.scan(step, state0, (qc, kc, vc))
    y = ys.transpose(1, 0, 3, 2, 4).reshape(B, S, RET_HEADS, RET_VAL_DIM)
    y = y * lax.rsqrt(jnp.mean(y * y, axis=-1, keepdims=True) + EPS)
    y = y.reshape(B, S, RET_HEADS * RET_VAL_DIM)
    y = (jax.nn.silu(g.astype(jnp.float32)) * y).astype(h.dtype)
    return y @ w_out


def sqrelu_mlp(h, w_up, w_down):
    a = jax.nn.relu(h @ w_up)
    return (a * a) @ w_down


def setup_inputs(seed: int = 0) -> dict:
    key = jax.random.key(seed)
    ks = jax.random.split(key, 16)
    f32 = jnp.float32

    def w(k, shape, fan_in):
        return jax.random.normal(k, shape, f32) * (fan_in ** -0.5)

    def gain(k, shape):
        return 1.0 + 0.05 * jax.random.normal(k, shape, f32)

    return {
        "x": jax.random.normal(ks[0], (BATCH, SEQ, D_MODEL), f32),
        "p": jax.random.normal(ks[1], (DEPTH, BATCH, SEQ, PLE_DIM), f32),
        "mix_norm": gain(ks[2], (DEPTH, D_MODEL)),
        "sb_w_in": w(ks[3], (N_SB, D_MODEL, 3 * D_MODEL), D_MODEL),
        "sb_w_out": w(ks[4], (N_SB, D_MODEL, D_MODEL), D_MODEL),
        "ret_w_in": w(ks[5], (N_RET, D_MODEL, 6 * D_MODEL), D_MODEL),
        "ret_w_out": w(ks[6], (N_RET, 2 * D_MODEL, D_MODEL), 2 * D_MODEL),
        "mlp_norm": gain(ks[7], (DEPTH, D_MODEL)),
        "mlp_w_up": w(ks[8], (DEPTH, D_MODEL, D_FF), D_MODEL),
        "mlp_w_down": w(ks[9], (DEPTH, D_FF, D_MODEL), D_FF),
        "ple_norm": gain(ks[10], (DEPTH, D_MODEL)),
        "ple_w_gate": w(ks[11], (DEPTH, D_MODEL, D_MODEL), D_MODEL),
        "ple_w_up": w(ks[12], (DEPTH, PLE_DIM, D_MODEL), PLE_DIM),
        "final_norm": gain(ks[13], (D_MODEL,)),
    }


def reference(x, p, mix_norm, sb_w_in, sb_w_out, ret_w_in, ret_w_out,
              mlp_norm, mlp_w_up, mlp_w_down, ple_norm, ple_w_gate, ple_w_up,
              final_norm):
    h = x
    for i in range(DEPTH):
        hn = rms_norm(h, mix_norm[i])
        if i % N_MIXERS == 0:
            h = h + stick_breaking_attention(hn, sb_w_in[i // N_MIXERS], sb_w_out[i // N_MIXERS])
        else:
            h = h + retention(hn, ret_w_in[i // N_MIXERS], ret_w_out[i // N_MIXERS])
        h = h + sqrelu_mlp(rms_norm(h, mlp_norm[i]), mlp_w_up[i], mlp_w_down[i])
        gate = jax.nn.sigmoid((rms_norm(h, ple_norm[i]) @ ple_w_gate[i]).astype(jnp.float32))
        h = h + ((p[i] @ ple_w_up[i]).astype(jnp.float32) * gate).astype(h.dtype)
    return rms_norm(h, final_norm).astype(x.dtype)
```

```python
import functools
import math

import jax
import jax.numpy as jnp
from jax import lax
from jax.experimental import pallas as pl
from jax.experimental.pallas import tpu as pltpu

F32 = jnp.float32
BF16 = jnp.bfloat16

EPS = 1e-6
SB_HEADS = 16
RET_HEADS = 4
ROPE_BASE = 10000.0

VMEM_LIMIT_BYTES = 56 * 1024 * 1024
LANES = 128

ROW_TILE = 512
MLP_ROW_TILE = 1024
MLP_FF_TILE = 1024
SB_TILE = 256
RET_CHUNK = 256


def _params(*sem):
    return pltpu.CompilerParams(dimension_semantics=sem, vmem_limit_bytes=VMEM_LIMIT_BYTES)


def _rms_norm(x, g):
    ms = jnp.mean(x * x, axis=-1, keepdims=True)
    return x * lax.rsqrt(ms + EPS) * g


def _norm_proj_kernel(x_ref, g_ref, w_ref, o_ref, xn_ref):
    @pl.when(pl.program_id(1) == 0)
    def _():
        xn_ref[...] = _rms_norm(x_ref[...], g_ref[...]).astype(BF16)

    o_ref[...] = jnp.dot(xn_ref[...], w_ref[...], preferred_element_type=F32).astype(o_ref.dtype)


def _norm_proj(h, g, w, *, tn):
    t, d = h.shape
    n = w.shape[1]
    tm = ROW_TILE
    return pl.pallas_call(
        _norm_proj_kernel,
        grid=(t // tm, n // tn),
        in_specs=[
            pl.BlockSpec((tm, d), lambda i, j: (i, 0)),
            pl.BlockSpec((1, d), lambda i, j: (0, 0)),
            pl.BlockSpec((d, tn), lambda i, j: (0, j)),
        ],
        out_specs=pl.BlockSpec((tm, tn), lambda i, j: (i, j)),
        out_shape=jax.ShapeDtypeStruct((t, n), BF16),
        scratch_shapes=[pltpu.VMEM((tm, d), BF16)],
        compiler_params=_params("parallel", "arbitrary"),
        name="norm_proj",
    )(h, g.reshape(1, d), w)


def _sb_kernel(q_ref, k_ref, v_ref, o_ref, acc_ref, carry_ref, *, tile, scale):
    qi = pl.program_id(2)
    half = LANES // 2
    lane = lax.broadcasted_iota(jnp.int32, (1, LANES), 1)
    head_lanes = (lane < half, lane >= half)

    q = q_ref[...]
    zero = jnp.zeros_like(q)
    q_heads = [jnp.where(m, q, zero) for m in head_lanes]

    row = lax.broadcasted_iota(jnp.int32, (tile, tile), 0)
    col = lax.broadcasted_iota(jnp.int32, (tile, tile), 1)
    strict = jnp.where(row > col, 1.0, 0.0).astype(BF16)
    causal = col < row

    acc_ref[...] = jnp.zeros_like(acc_ref)
    carry_ref[...] = jnp.zeros_like(carry_ref)

    def step(kj, masked):
        ks = pl.multiple_of(kj * tile, tile)
        kt = k_ref[pl.ds(ks, tile), :]
        vt = v_ref[pl.ds(ks, tile), :]
        vzero = jnp.zeros_like(vt)
        out = None
        for h in range(2):
            z = lax.dot_general(q_heads[h], kt, (((1,), (1,)), ((), ())),
                                preferred_element_type=F32) * scale
            sp = jnp.maximum(z, 0.0) + jnp.log(1.0 + jnp.exp(-jnp.abs(z)))
            log1m = jnp.where(causal, -sp, 0.0) if masked else -sp
            hi = log1m.astype(BF16)
            lo = (log1m - hi.astype(F32)).astype(BF16)
            suffix = (jnp.dot(hi, strict, preferred_element_type=F32)
                      + jnp.dot(lo, strict, preferred_element_type=F32))
            carry = carry_ref[h]
            a = jnp.exp((z - sp) + suffix + carry)
            if masked:
                a = jnp.where(causal, a, 0.0)
            carry_ref[h] = carry + jnp.sum(log1m, axis=1, keepdims=True)
            vh = jnp.where(head_lanes[h], vt, vzero)
            contrib = jnp.dot(a.astype(BF16), vh, preferred_element_type=F32)
            out = contrib if out is None else out + contrib
        acc_ref[...] += out

    step(qi, True)

    def body(i, c):
        step(qi - 1 - i, False)
        return c

    lax.fori_loop(0, qi, body, 0)
    o_ref[...] = acc_ref[...].astype(o_ref.dtype)


def _sb_attention(qkv, *, batch, seq, d_model):
    tile = SB_TILE
    head_dim = d_model // SB_HEADS
    pairs = SB_HEADS // 2
    qkv3 = qkv.reshape(batch, seq, 3 * d_model)
    kern = functools.partial(_sb_kernel, tile=tile, scale=head_dim ** -0.5)
    out = pl.pallas_call(
        kern,
        grid=(batch, pairs, seq // tile),
        in_specs=[
            pl.BlockSpec((None, tile, LANES), lambda b, p, i: (b, i, p)),
            pl.BlockSpec((None, seq, LANES), lambda b, p, i: (b, 0, pairs + p)),
            pl.BlockSpec((None, seq, LANES), lambda b, p, i: (b, 0, 2 * pairs + p)),
        ],
        out_specs=pl.BlockSpec((None, tile, LANES), lambda b, p, i: (b, i, p)),
        out_shape=jax.ShapeDtypeStruct((batch, seq, d_model), BF16),
        scratch_shapes=[
            pltpu.VMEM((tile, LANES), F32),
            pltpu.VMEM((2, tile, 1), F32),
        ],
        compiler_params=_params("parallel", "parallel", "arbitrary"),
        name="sb_attention",
    )(qkv3, qkv3, qkv3)
    return out.reshape(batch * seq, d_model)


def _ret_kernel(q_ref, k_ref, v_ref, g_ref, cos_ref, sin_ref, intra_ref, cross_ref,
                kv_ref, cd_ref, o_ref, state_ref, *, k_scale):
    @pl.when(pl.program_id(2) == 0)
    def _():
        state_ref[...] = jnp.zeros_like(state_ref)

    cos = cos_ref[...]
    sin = sin_ref[...]
    hd = cos.shape[1]

    def rope(x_ref):
        x = x_ref[...].astype(F32)
        x1, x2 = x[:, :hd], x[:, hd:]
        return x1 * cos - x2 * sin, x1 * sin + x2 * cos

    q1, q2 = rope(q_ref)
    k1, k2 = rope(k_ref)
    k1, k2 = k1 * k_scale, k2 * k_scale
    v = v_ref[...]
    cross_d = cross_ref[...]
    kv_d = kv_ref[...]

    qr = jnp.concatenate([q1, q2], axis=1).astype(BF16)
    kr = jnp.concatenate([k1, k2], axis=1).astype(BF16)
    scores = lax.dot_general(qr, kr, (((1,), (1,)), ((), ())),
                             preferred_element_type=F32) * intra_ref[...]
    inner = jnp.dot(scores.astype(BF16), v, preferred_element_type=F32)

    state = state_ref[...]
    qc = jnp.concatenate([q1 * cross_d, q2 * cross_d], axis=1).astype(BF16)
    cross = jnp.dot(qc, state.astype(BF16), preferred_element_type=F32)

    kd = jnp.concatenate([k1 * kv_d, k2 * kv_d], axis=1).astype(BF16)
    state_ref[...] = state * cd_ref[...] + lax.dot_general(
        kd, v, (((0,), (0,)), ((), ())), preferred_element_type=F32)

    y = inner + cross
    y = y * lax.rsqrt(jnp.mean(y * y, axis=-1, keepdims=True) + EPS)
    g = g_ref[...].astype(F32)
    o_ref[...] = (g * jax.nn.sigmoid(g) * y).astype(o_ref.dtype)


def _retention_tables(seq, key_dim, val_dim, chunk):
    half = key_dim // 2
    inv_freq = ROPE_BASE ** (-jnp.arange(0, key_dim, 2, dtype=F32) / key_dim)
    ang = jnp.arange(seq).astype(F32)[:, None] * inv_freq[None, :]
    log_gamma = jnp.log1p(-jnp.exp2(-5.0 - jnp.arange(RET_HEADS, dtype=F32)))
    idx = jnp.arange(chunk, dtype=F32)
    diff = idx[:, None] - idx[None, :]
    intra = jnp.where(diff >= 0, jnp.exp(log_gamma[:, None, None] * jnp.maximum(diff, 0.0)), 0.0)
    cross = jnp.exp(log_gamma[:, None] * (idx + 1.0))
    kv = jnp.exp(log_gamma[:, None] * (chunk - 1.0 - idx))
    cd = jnp.exp(log_gamma * chunk)
    return dict(
        cos=jnp.cos(ang), sin=jnp.sin(ang), intra=intra,
        cross=jnp.broadcast_to(cross[:, :, None], (RET_HEADS, chunk, half)),
        kv=jnp.broadcast_to(kv[:, :, None], (RET_HEADS, chunk, half)),
        cd=jnp.broadcast_to(cd[:, None, None], (RET_HEADS, 1, val_dim)),
    )


def _retention(proj, *, batch, seq, d_model):
    chunk = RET_CHUNK
    key_dim = d_model // RET_HEADS
    val_dim = 2 * d_model // RET_HEADS
    half = key_dim // 2
    tb = _retention_tables(seq, key_dim, val_dim, chunk)
    proj3 = proj.reshape(batch, seq, 6 * d_model)
    nh = RET_HEADS
    kern = functools.partial(_ret_kernel, k_scale=key_dim ** -0.5)
    out = pl.pallas_call(
        kern,
        grid=(batch, nh, seq // chunk),
        in_specs=[
            pl.BlockSpec((None, chunk, key_dim), lambda b, h, c: (b, c, h)),
            pl.BlockSpec((None, chunk, key_dim), lambda b, h, c: (b, c, nh + h)),
            pl.BlockSpec((None, chunk, val_dim), lambda b, h, c: (b, c, nh + h)),
            pl.BlockSpec((None, chunk, val_dim), lambda b, h, c: (b, c, 2 * nh + h)),
            pl.BlockSpec((chunk, half), lambda b, h, c: (c, 0)),
            pl.BlockSpec((chunk, half), lambda b, h, c: (c, 0)),
            pl.BlockSpec((None, chunk, chunk), lambda b, h, c: (h, 0, 0)),
            pl.BlockSpec((None, chunk, half), lambda b, h, c: (h, 0, 0)),
            pl.BlockSpec((None, chunk, half), lambda b, h, c: (h, 0, 0)),
            pl.BlockSpec((None, 1, val_dim), lambda b, h, c: (h, 0, 0)),
        ],
        out_specs=pl.BlockSpec((None, chunk, val_dim), lambda b, h, c: (b, c, h)),
        out_shape=jax.ShapeDtypeStruct((batch, seq, nh * val_dim), BF16),
        scratch_shapes=[pltpu.VMEM((key_dim, val_dim), F32)],
        compiler_params=_params("parallel", "parallel", "arbitrary"),
        name="retention",
    )(proj3, proj3, proj3, proj3, tb["cos"], tb["sin"], tb["intra"], tb["cross"], tb["kv"], tb["cd"])
    return out.reshape(batch * seq, nh * val_dim)


def _out_proj_kernel(h_ref, x_ref, w_ref, o_ref):
    o_ref[...] = h_ref[...] + jnp.dot(x_ref[...], w_ref[...], preferred_element_type=F32)


def _out_proj(h, x, w):
    t, d = h.shape
    dk = x.shape[1]
    tm = ROW_TILE
    return pl.pallas_call(
        _out_proj_kernel,
        grid=(t // tm,),
        in_specs=[
            pl.BlockSpec((tm, d), lambda i: (i, 0)),
            pl.BlockSpec((tm, dk), lambda i: (i, 0)),
            pl.BlockSpec((dk, d), lambda i: (0, 0)),
        ],
        out_specs=pl.BlockSpec((tm, d), lambda i: (i, 0)),
        out_shape=jax.ShapeDtypeStruct((t, d), F32),
        compiler_params=_params("parallel"),
        name="out_proj",
    )(h, x, w)


def _mlp_kernel(h_ref, g_ref, wu_ref, wd_ref, o_ref, xn_ref):
    j = pl.program_id(1)

    @pl.when(j == 0)
    def _():
        h = h_ref[...]
        xn_ref[...] = _rms_norm(h, g_ref[...]).astype(BF16)
        o_ref[...] = h

    a = jnp.maximum(jnp.dot(xn_ref[...], wu_ref[...], preferred_element_type=F32), 0.0)
    o_ref[...] += jnp.dot((a * a).astype(BF16), wd_ref[...], preferred_element_type=F32)


def _mlp(h, g, w_up, w_down):
    t, d = h.shape
    f = w_up.shape[1]
    tm, tf = MLP_ROW_TILE, MLP_FF_TILE
    return pl.pallas_call(
        _mlp_kernel,
        grid=(t // tm, f // tf),
        in_specs=[
            pl.BlockSpec((tm, d), lambda i, j: (i, 0)),
            pl.BlockSpec((1, d), lambda i, j: (0, 0)),
            pl.BlockSpec((d, tf), lambda i, j: (0, j)),
            pl.BlockSpec((tf, d), lambda i, j: (j, 0)),
        ],
        out_specs=pl.BlockSpec((tm, d), lambda i, j: (i, 0)),
        out_shape=jax.ShapeDtypeStruct((t, d), F32),
        scratch_shapes=[pltpu.VMEM((tm, d), BF16)],
        compiler_params=_params("parallel", "arbitrary"),
        name="mlp",
    )(h, g.reshape(1, d), w_up, w_down)


def _ple_kernel(h_ref, p_ref, g_ref, wg_ref, wu_ref, fg_ref, o_ref, *, final_norm):
    h = h_ref[...]
    xn = _rms_norm(h, g_ref[...]).astype(BF16)
    gate = jax.nn.sigmoid(jnp.dot(xn, wg_ref[...], preferred_element_type=F32))
    up = jnp.dot(p_ref[...].astype(BF16), wu_ref[...], preferred_element_type=F32)
    out = h + up * gate
    if final_norm:
        out = _rms_norm(out, fg_ref[...])
    o_ref[...] = out


def _ple(h, p, g, w_gate, w_up, final_g, *, final_norm):
    t, d = h.shape
    dp = p.shape[1]
    tm = ROW_TILE
    return pl.pallas_call(
        functools.partial(_ple_kernel, final_norm=final_norm),
        grid=(t // tm,),
        in_specs=[
            pl.BlockSpec((tm, d), lambda i: (i, 0)),
            pl.BlockSpec((tm, dp), lambda i: (i, 0)),
            pl.BlockSpec((1, d), lambda i: (0, 0)),
            pl.BlockSpec((d, d), lambda i: (0, 0)),
            pl.BlockSpec((dp, d), lambda i: (0, 0)),
            pl.BlockSpec((1, d), lambda i: (0, 0)),
        ],
        out_specs=pl.BlockSpec((tm, d), lambda i: (i, 0)),
        out_shape=jax.ShapeDtypeStruct((t, d), F32),
        compiler_params=_params("parallel"),
        name="ple",
    )(h, p, g.reshape(1, d), w_gate, w_up, final_g.reshape(1, d))


def kernel(x, p, mix_norm, sb_w_in, sb_w_out, ret_w_in, ret_w_out, mlp_norm, mlp_w_up,
           mlp_w_down, ple_norm, ple_w_gate, ple_w_up, final_norm):
    batch, seq, d_model = x.shape
    depth = p.shape[0]
    t = batch * seq
    h = x.reshape(t, d_model)
    p2 = p.reshape(depth, t, p.shape[-1])
    for i in range(depth):
        if i % 2 == 0:
            qkv = _norm_proj(h, mix_norm[i], sb_w_in[i // 2].astype(BF16), tn=1024)
            mix = _sb_attention(qkv, batch=batch, seq=seq, d_model=d_model)
            w_out = sb_w_out[i // 2]
        else:
            proj = _norm_proj(h, mix_norm[i], ret_w_in[i // 2].astype(BF16), tn=1024)
            mix = _retention(proj, batch=batch, seq=seq, d_model=d_model)
            w_out = ret_w_out[i // 2]
        h = _out_proj(h, mix, w_out.astype(BF16))
        h = _mlp(h, mlp_norm[i], mlp_w_up[i].astype(BF16), mlp_w_down[i].astype(BF16))
        h = _ple(h, p2[i], ple_norm[i], ple_w_gate[i].astype(BF16), ple_w_up[i].astype(BF16),
                 final_norm, final_norm=(i == depth - 1))
    return h.reshape(batch, seq, d_model)
```

```python
import functools
import math

import jax
import jax.numpy as jnp
from jax import lax
from jax.experimental import pallas as pl
from jax.experimental.pallas import tpu as pltpu

F32 = jnp.float32
BF16 = jnp.bfloat16

EPS = 1e-6
SB_HEADS = 16
RET_HEADS = 4
ROPE_BASE = 10000.0

VMEM_LIMIT_BYTES = 56 * 1024 * 1024
LANES = 128
LOG2E = math.log2(math.e)
MASK_BIG = 1e30
PZ_MAX = 126.0

ROW_TILE = 512
PROJ_ROW_TILE = 1024
PROJ_COL_TILE = 1536
MLP_ROW_TILE = 1024
MLP_FF_TILE = 1024
SB_TILE = 256
RET_CHUNK = 256


def _params(*sem):
    return pltpu.CompilerParams(dimension_semantics=sem, vmem_limit_bytes=VMEM_LIMIT_BYTES)


def _rms_norm(x, g):
    ms = jnp.mean(x * x, axis=-1, keepdims=True)
    return x * lax.rsqrt(ms + EPS) * g


def _norm_proj_kernel(x_ref, g_ref, w_ref, cs_ref, o_ref, xn_ref):
    @pl.when(pl.program_id(1) == 0)
    def _():
        xn_ref[...] = _rms_norm(x_ref[...], g_ref[...]).astype(BF16)

    acc = jnp.dot(xn_ref[...], w_ref[...], preferred_element_type=F32)
    o_ref[...] = (acc * cs_ref[...]).astype(o_ref.dtype)


def _norm_proj(h, g, w, col_scale):
    t, d = h.shape
    n = w.shape[1]
    tm, tn = PROJ_ROW_TILE, PROJ_COL_TILE
    return pl.pallas_call(
        _norm_proj_kernel,
        grid=(t // tm, n // tn),
        in_specs=[
            pl.BlockSpec((tm, d), lambda i, j: (i, 0)),
            pl.BlockSpec((1, d), lambda i, j: (0, 0)),
            pl.BlockSpec((d, tn), lambda i, j: (0, j)),
            pl.BlockSpec((1, tn), lambda i, j: (0, j)),
        ],
        out_specs=pl.BlockSpec((tm, tn), lambda i, j: (i, j)),
        out_shape=jax.ShapeDtypeStruct((t, n), BF16),
        scratch_shapes=[pltpu.VMEM((tm, d), BF16)],
        compiler_params=_params("parallel", "arbitrary"),
        name="norm_proj",
    )(h, g.reshape(1, d), w, col_scale.reshape(1, n))


def _sb_kernel(qi_tbl, kj_tbl, q_ref, k_ref, v_ref, o_ref,
               pz_raw, pz_msk, sfx, mask_ref, incl_ref, acc_ref, carry_ref, *, tile, n_iter):
    half = LANES // 2
    lane = lax.broadcasted_iota(jnp.int32, (1, LANES), 1)
    head0 = lane < half

    row = lax.broadcasted_iota(jnp.int32, (tile, tile), 0)
    col = lax.broadcasted_iota(jnp.int32, (tile, tile), 1)
    incl_ref[...] = jnp.where(row >= col, -1.0, 0.0).astype(BF16)
    diag_mask = jnp.where(col >= row, -MASK_BIG, PZ_MAX)
    mask_ref[0] = jnp.full((2 * tile, tile), PZ_MAX, F32)
    mask_ref[1] = jnp.concatenate([diag_mask, diag_mask], axis=0)
    for par in range(2):
        pz_raw[par] = jnp.zeros((2 * tile, tile), F32)
        pz_msk[par] = jnp.zeros((2 * tile, tile), F32)
        sfx[par] = jnp.zeros((2 * tile, tile), F32)
    acc_ref[...] = jnp.zeros_like(acc_ref)
    carry_ref[...] = jnp.zeros_like(carry_ref)

    def rows(idx):
        return pl.ds(pl.multiple_of(idx * tile, tile), tile)

    def sub_step(s, par):
        xq, xk = qi_tbl[s + 2], kj_tbl[s + 2]
        q = q_ref[rows(xq), :]
        qzero = jnp.zeros_like(q)
        q2 = jnp.concatenate([jnp.where(head0, q, qzero), jnp.where(head0, qzero, q)], axis=0)
        pz_raw[par] = lax.dot_general(q2, k_ref[rows(xk), :], (((1,), (1,)), ((), ())),
                                      preferred_element_type=F32)

        yq, yk = qi_tbl[s + 1], kj_tbl[s + 1]
        pz = jnp.minimum(pz_raw[1 - par], mask_ref[(yq == yk).astype(jnp.int32)])
        l = jnp.log(1.0 + jnp.exp2(pz)) * LOG2E
        sfx[1 - par] = jnp.dot(l.astype(BF16), incl_ref[...], preferred_element_type=F32)
        pz_msk[1 - par] = pz

        zq, zk = qi_tbl[s], kj_tbl[s]
        first = zq == zk
        carry = jnp.where(first, 0.0, carry_ref[...])
        suffix = sfx[par]
        a = jnp.exp2(suffix + pz_msk[par] + carry)
        carry_ref[...] = carry + suffix[:, 0:1]
        res = jnp.dot(a.astype(BF16), v_ref[rows(zk), :], preferred_element_type=F32)
        acc = jnp.where(head0, res[:tile], res[tile:]) + jnp.where(first, 0.0, acc_ref[...])
        acc_ref[...] = acc
        o_ref[rows(zq), :] = acc.astype(o_ref.dtype)

    def body(i, c):
        sub_step(2 * i, 0)
        sub_step(2 * i + 1, 1)
        return c

    lax.fori_loop(0, n_iter // 2, body, 0)


def _sb_tables(nq):
    qi = [t for t in range(nq) for _ in range(t + 1)]
    kj = [j for t in range(nq) for j in range(t, -1, -1)]
    lead = 2 + len(qi) % 2
    n_iter = lead + len(qi)
    qi = [0] * lead + qi + [0, 0]
    kj = [0] * lead + kj + [0, 0]
    return jnp.asarray(qi, jnp.int32), jnp.asarray(kj, jnp.int32), n_iter


def _sb_attention(qkv, *, batch, seq, d_model):
    tile = SB_TILE
    pairs = SB_HEADS // 2
    qkv3 = qkv.reshape(batch, seq, 3 * d_model)
    qi_tbl, kj_tbl, n_iter = _sb_tables(seq // tile)
    kern = functools.partial(_sb_kernel, tile=tile, n_iter=n_iter)
    grid_spec = pltpu.PrefetchScalarGridSpec(
        num_scalar_prefetch=2,
        grid=(batch, pairs),
        in_specs=[
            pl.BlockSpec((None, seq, LANES), lambda b, p, *_: (b, 0, p)),
            pl.BlockSpec((None, seq, LANES), lambda b, p, *_: (b, 0, pairs + p)),
            pl.BlockSpec((None, seq, LANES), lambda b, p, *_: (b, 0, 2 * pairs + p)),
        ],
        out_specs=pl.BlockSpec((None, seq, LANES), lambda b, p, *_: (b, 0, p)),
        scratch_shapes=[
            pltpu.VMEM((2, 2 * tile, tile), F32),
            pltpu.VMEM((2, 2 * tile, tile), F32),
            pltpu.VMEM((2, 2 * tile, tile), F32),
            pltpu.VMEM((2, 2 * tile, tile), F32),
            pltpu.VMEM((tile, tile), BF16),
            pltpu.VMEM((tile, LANES), F32),
            pltpu.VMEM((2 * tile, 1), F32),
        ],
    )
    out = pl.pallas_call(
        kern,
        grid_spec=grid_spec,
        out_shape=jax.ShapeDtypeStruct((batch, seq, d_model), BF16),
        compiler_params=_params("parallel", "parallel"),
        name="sb_attention",
    )(qi_tbl, kj_tbl, qkv3, qkv3, qkv3)
    return out.reshape(batch * seq, d_model)


def _ret_kernel(q_ref, k_ref, v_ref, g_ref, cos_ref, sin_ref, intra_ref, cross_ref,
                kv_ref, cd_ref, o_ref, state_ref, *, k_scale):
    @pl.when(pl.program_id(2) == 0)
    def _():
        state_ref[...] = jnp.zeros_like(state_ref)

    cos = cos_ref[...]
    sin = sin_ref[...]
    hd = cos.shape[1]

    def rope(x_ref):
        x = x_ref[...].astype(F32)
        x1, x2 = x[:, :hd], x[:, hd:]
        return x1 * cos - x2 * sin, x1 * sin + x2 * cos

    q1, q2 = rope(q_ref)
    k1, k2 = rope(k_ref)
    k1, k2 = k1 * k_scale, k2 * k_scale
    v = v_ref[...]
    cross_d = cross_ref[...]
    kv_d = kv_ref[...]

    qr = jnp.concatenate([q1, q2], axis=1).astype(BF16)
    kr = jnp.concatenate([k1, k2], axis=1).astype(BF16)
    scores = lax.dot_general(qr, kr, (((1,), (1,)), ((), ())),
                             preferred_element_type=F32) * intra_ref[...]
    inner = jnp.dot(scores.astype(BF16), v, preferred_element_type=F32)

    state = state_ref[...]
    qc = jnp.concatenate([q1 * cross_d, q2 * cross_d], axis=1).astype(BF16)
    cross = jnp.dot(qc, state.astype(BF16), preferred_element_type=F32)

    kd = jnp.concatenate([k1 * kv_d, k2 * kv_d], axis=1).astype(BF16)
    state_ref[...] = state * cd_ref[...] + lax.dot_general(
        kd, v, (((0,), (0,)), ((), ())), preferred_element_type=F32)

    y = inner + cross
    y = y * lax.rsqrt(jnp.mean(y * y, axis=-1, keepdims=True) + EPS)
    g = g_ref[...].astype(F32)
    o_ref[...] = (g * jax.nn.sigmoid(g) * y).astype(o_ref.dtype)


def _retention_tables(seq, key_dim, val_dim, chunk):
    half = key_dim // 2
    inv_freq = ROPE_BASE ** (-jnp.arange(0, key_dim, 2, dtype=F32) / key_dim)
    ang = jnp.arange(seq).astype(F32)[:, None] * inv_freq[None, :]
    log_gamma = jnp.log1p(-jnp.exp2(-5.0 - jnp.arange(RET_HEADS, dtype=F32)))
    idx = jnp.arange(chunk, dtype=F32)
    diff = idx[:, None] - idx[None, :]
    intra = jnp.where(diff >= 0, jnp.exp(log_gamma[:, None, None] * jnp.maximum(diff, 0.0)), 0.0)
    cross = jnp.exp(log_gamma[:, None] * (idx + 1.0))
    kv = jnp.exp(log_gamma[:, None] * (chunk - 1.0 - idx))
    cd = jnp.exp(log_gamma * chunk)
    return dict(
        cos=jnp.cos(ang), sin=jnp.sin(ang), intra=intra,
        cross=jnp.broadcast_to(cross[:, :, None], (RET_HEADS, chunk, half)),
        kv=jnp.broadcast_to(kv[:, :, None], (RET_HEADS, chunk, half)),
        cd=jnp.broadcast_to(cd[:, None, None], (RET_HEADS, 1, val_dim)),
    )


def _retention(proj, *, batch, seq, d_model):
    chunk = RET_CHUNK
    key_dim = d_model // RET_HEADS
    val_dim = 2 * d_model // RET_HEADS
    half = key_dim // 2
    tb = _retention_tables(seq, key_dim, val_dim, chunk)
    proj3 = proj.reshape(batch, seq, 6 * d_model)
    nh = RET_HEADS
    kern = functools.partial(_ret_kernel, k_scale=key_dim ** -0.5)
    out = pl.pallas_call(
        kern,
        grid=(batch, nh, seq // chunk),
        in_specs=[
            pl.BlockSpec((None, chunk, key_dim), lambda b, h, c: (b, c, h)),
            pl.BlockSpec((None, chunk, key_dim), lambda b, h, c: (b, c, nh + h)),
            pl.BlockSpec((None, chunk, val_dim), lambda b, h, c: (b, c, nh + h)),
            pl.BlockSpec((None, chunk, val_dim), lambda b, h, c: (b, c, 2 * nh + h)),
            pl.BlockSpec((chunk, half), lambda b, h, c: (c, 0)),
            pl.BlockSpec((chunk, half), lambda b, h, c: (c, 0)),
            pl.BlockSpec((None, chunk, chunk), lambda b, h, c: (h, 0, 0)),
            pl.BlockSpec((None, chunk, half), lambda b, h, c: (h, 0, 0)),
            pl.BlockSpec((None, chunk, half), lambda b, h, c: (h, 0, 0)),
            pl.BlockSpec((None, 1, val_dim), lambda b, h, c: (h, 0, 0)),
        ],
        out_specs=pl.BlockSpec((None, chunk, val_dim), lambda b, h, c: (b, c, h)),
        out_shape=jax.ShapeDtypeStruct((batch, seq, nh * val_dim), BF16),
        scratch_shapes=[pltpu.VMEM((key_dim, val_dim), F32)],
        compiler_params=_params("parallel", "parallel", "arbitrary"),
        name="retention",
    )(proj3, proj3, proj3, proj3, tb["cos"], tb["sin"], tb["intra"], tb["cross"], tb["kv"], tb["cd"])
    return out.reshape(batch * seq, nh * val_dim)


def _out_proj_kernel(h_ref, x_ref, w_ref, o_ref):
    o_ref[...] = h_ref[...] + jnp.dot(x_ref[...], w_ref[...], preferred_element_type=F32)


def _out_proj(h, x, w):
    t, d = h.shape
    dk = x.shape[1]
    tm = ROW_TILE
    return pl.pallas_call(
        _out_proj_kernel,
        grid=(t // tm,),
        in_specs=[
            pl.BlockSpec((tm, d), lambda i: (i, 0)),
            pl.BlockSpec((tm, dk), lambda i: (i, 0)),
            pl.BlockSpec((dk, d), lambda i: (0, 0)),
        ],
        out_specs=pl.BlockSpec((tm, d), lambda i: (i, 0)),
        out_shape=jax.ShapeDtypeStruct((t, d), F32),
        compiler_params=_params("parallel"),
        name="out_proj",
    )(h, x, w)


def _mlp_kernel(h_ref, g_ref, wu_ref, wd_ref, o_ref, xn_ref):
    j = pl.program_id(1)

    @pl.when(j == 0)
    def _():
        h = h_ref[...]
        xn_ref[...] = _rms_norm(h, g_ref[...]).astype(BF16)
        o_ref[...] = h

    a = jnp.maximum(jnp.dot(xn_ref[...], wu_ref[...], preferred_element_type=F32), 0.0)
    o_ref[...] += jnp.dot((a * a).astype(BF16), wd_ref[...], preferred_element_type=F32)


def _mlp(h, g, w_up, w_down):
    t, d = h.shape
    f = w_up.shape[1]
    tm, tf = MLP_ROW_TILE, MLP_FF_TILE
    return pl.pallas_call(
        _mlp_kernel,
        grid=(t // tm, f // tf),
        in_specs=[
            pl.BlockSpec((tm, d), lambda i, j: (i, 0)),
            pl.BlockSpec((1, d), lambda i, j: (0, 0)),
            pl.BlockSpec((d, tf), lambda i, j: (0, j)),
            pl.BlockSpec((tf, d), lambda i, j: (j, 0)),
        ],
        out_specs=pl.BlockSpec((tm, d), lambda i, j: (i, 0)),
        out_shape=jax.ShapeDtypeStruct((t, d), F32),
        scratch_shapes=[pltpu.VMEM((tm, d), BF16)],
        compiler_params=_params("parallel", "arbitrary"),
        name="mlp",
    )(h, g.reshape(1, d), w_up, w_down)


def _ple_kernel(h_ref, p_ref, g_ref, wg_ref, wu_ref, fg_ref, o_ref, *, final_norm):
    h = h_ref[...]
    xn = _rms_norm(h, g_ref[...]).astype(BF16)
    gate = jax.nn.sigmoid(jnp.dot(xn, wg_ref[...], preferred_element_type=F32))
    up = jnp.dot(p_ref[...].astype(BF16), wu_ref[...], preferred_element_type=F32)
    out = h + up * gate
    if final_norm:
        out = _rms_norm(out, fg_ref[...])
    o_ref[...] = out


def _ple(h, p, g, w_gate, w_up, final_g, *, final_norm):
    t, d = h.shape
    dp = p.shape[1]
    tm = ROW_TILE
    return pl.pallas_call(
        functools.partial(_ple_kernel, final_norm=final_norm),
        grid=(t // tm,),
        in_specs=[
            pl.BlockSpec((tm, d), lambda i: (i, 0)),
            pl.BlockSpec((tm, dp), lambda i: (i, 0)),
            pl.BlockSpec((1, d), lambda i: (0, 0)),
            pl.BlockSpec((d, d), lambda i: (0, 0)),
            pl.BlockSpec((dp, d), lambda i: (0, 0)),
            pl.BlockSpec((1, d), lambda i: (0, 0)),
        ],
        out_specs=pl.BlockSpec((tm, d), lambda i: (i, 0)),
        out_shape=jax.ShapeDtypeStruct((t, d), F32),
        compiler_params=_params("parallel"),
        name="ple",
    )(h, p, g.reshape(1, d), w_gate, w_up, final_g.reshape(1, d))


def kernel(x, p, mix_norm, sb_w_in, sb_w_out, ret_w_in, ret_w_out, mlp_norm, mlp_w_up,
           mlp_w_down, ple_norm, ple_w_gate, ple_w_up, final_norm):
    batch, seq, d_model = x.shape
    depth = p.shape[0]
    t = batch * seq
    h = x.reshape(t, d_model)
    p2 = p.reshape(depth, t, p.shape[-1])
    for i in range(depth):
        if i % 2 == 0:
            q_scale = LOG2E * (d_model // SB_HEADS) ** -0.5
            col_scale = jnp.concatenate([jnp.full((d_model,), q_scale, F32), jnp.ones((2 * d_model,), F32)])
            qkv = _norm_proj(h, mix_norm[i], sb_w_in[i // 2].astype(BF16), col_scale)
            mix = _sb_attention(qkv, batch=batch, seq=seq, d_model=d_model)
            w_out = sb_w_out[i // 2]
        else:
            proj = _norm_proj(h, mix_norm[i], ret_w_in[i // 2].astype(BF16), jnp.ones((6 * d_model,), F32))
            mix = _retention(proj, batch=batch, seq=seq, d_model=d_model)
            w_out = ret_w_out[i // 2]
        h = _out_proj(h, mix, w_out.astype(BF16))
        h = _mlp(h, mlp_norm[i], mlp_w_up[i].astype(BF16), mlp_w_down[i].astype(BF16))
        h = _ple(h, p2[i], ple_norm[i], ple_w_gate[i].astype(BF16), ple_w_up[i].astype(BF16),
                 final_norm, final_norm=(i == depth - 1))
    return h.reshape(batch, seq, d_model)
```

```python
import functools
import math

import jax
import jax.numpy as jnp
from jax import lax
from jax.experimental import pallas as pl
from jax.experimental.pallas import tpu as pltpu

F32 = jnp.float32
BF16 = jnp.bfloat16

EPS = 1e-6
SB_HEADS = 16
RET_HEADS = 4
ROPE_BASE = 10000.0

VMEM_LIMIT_BYTES = 56 * 1024 * 1024
LANES = 128
LOG2E = math.log2(math.e)
MASK_BIG = 1e30
PZ_MAX = 126.0
SKIP_LOG2 = -160.0

PROJ_ROW_TILE = 1024
PROJ_COL_TILE = 1536
MLP_ROW_TILE = 1024
MLP_FF_TILE = 1024
SB_TILE = 256
RET_CHUNK = 256
RET_CHUNKS_PER_STEP = 4


def _params(*sem):
    return pltpu.CompilerParams(dimension_semantics=sem, vmem_limit_bytes=VMEM_LIMIT_BYTES)


def _rms_norm(x, g):
    ms = jnp.mean(x * x, axis=-1, keepdims=True)
    return x * lax.rsqrt(ms + EPS) * g


def _norm_proj_kernel(x_ref, g_ref, w_ref, cs_ref, o_ref, xn_ref):
    @pl.when(pl.program_id(1) == 0)
    def _():
        xn_ref[...] = _rms_norm(x_ref[...], g_ref[...]).astype(BF16)

    acc = jnp.dot(xn_ref[...], w_ref[...], preferred_element_type=F32)
    o_ref[...] = (acc * cs_ref[...]).astype(o_ref.dtype)


def _norm_proj(h, g, w, col_scale):
    t, d = h.shape
    n = w.shape[1]
    tm, tn = PROJ_ROW_TILE, PROJ_COL_TILE
    return pl.pallas_call(
        _norm_proj_kernel,
        grid=(t // tm, n // tn),
        in_specs=[
            pl.BlockSpec((tm, d), lambda i, j: (i, 0)),
            pl.BlockSpec((1, d), lambda i, j: (0, 0)),
            pl.BlockSpec((d, tn), lambda i, j: (0, j)),
            pl.BlockSpec((1, tn), lambda i, j: (0, j)),
        ],
        out_specs=pl.BlockSpec((tm, tn), lambda i, j: (i, j)),
        out_shape=jax.ShapeDtypeStruct((t, n), BF16),
        scratch_shapes=[pltpu.VMEM((tm, d), BF16)],
        compiler_params=_params("parallel", "arbitrary"),
        name="norm_proj",
    )(h, g.reshape(1, d), w, col_scale.reshape(1, n))


def _sb_kernel(q_ref, k_ref, v_ref, o_ref, pz_buf, sfx_buf, mask_ref, incl_ref, acc_ref,
               carry_ref, lst_ref, done_ref, *, tile, nq):
    half = LANES // 2
    lane = lax.broadcasted_iota(jnp.int32, (1, LANES), 1)
    head0 = lane < half

    row = lax.broadcasted_iota(jnp.int32, (tile, tile), 0)
    col = lax.broadcasted_iota(jnp.int32, (tile, tile), 1)
    incl_ref[...] = jnp.where(row >= col, -1.0, 0.0).astype(BF16)
    diag_mask = jnp.where(col >= row, -MASK_BIG, PZ_MAX)
    mask_ref[0] = jnp.full((2 * tile, tile), PZ_MAX, F32)
    mask_ref[1] = jnp.concatenate([diag_mask, diag_mask], axis=0)
    for par in range(2):
        pz_buf[par] = jnp.zeros((2 * tile, tile), F32)
        sfx_buf[par] = jnp.zeros((2 * tile, tile), F32)
    acc_ref[...] = jnp.zeros_like(acc_ref)
    carry_ref[...] = jnp.zeros_like(carry_ref)
    for i in range(nq + 1):
        done_ref[i] = 0

    def rows(idx):
        return pl.ds(pl.multiple_of(idx * tile, tile), tile)

    def entry(e, d):
        st = lst_ref[e]
        dummy = st == nq
        qi = jnp.minimum(st, nq - 1)
        kj = jnp.where(dummy, 0, qi - d)
        return st, dummy, qi, kj

    def sub_step(e, par, d):
        diag = d == 0

        st, dummy, _, kj = entry(e, d)
        carry = jnp.where(diag | dummy, 0.0, carry_ref[st])
        suffix = sfx_buf[par]
        a = jnp.exp2(suffix + pz_buf[par] + carry)
        carry = carry + suffix[:, 0:1]
        carry_ref[st] = carry
        done_ref[st] = (jnp.max(carry) < SKIP_LOG2).astype(jnp.int32)
        res = jnp.dot(a.astype(BF16), v_ref[rows(kj), :], preferred_element_type=F32)
        acc_ref[st] = (jnp.where(head0, res[:tile], res[tile:])
                       + jnp.where(diag | dummy, 0.0, acc_ref[st]))

        pz = pz_buf[1 - par]
        l = jnp.log(1.0 + jnp.exp2(pz)) * LOG2E
        sfx_buf[1 - par] = jnp.dot(l.astype(BF16), incl_ref[...], preferred_element_type=F32)

        _, _, qi, kj = entry(e + 2, d)
        q = q_ref[rows(qi), :]
        qzero = jnp.zeros_like(q)
        q2 = jnp.concatenate([jnp.where(head0, q, qzero), jnp.where(head0, qzero, q)], axis=0)
        pz_raw = lax.dot_general(q2, k_ref[rows(kj), :], (((1,), (1,)), ((), ())),
                                 preferred_element_type=F32)
        pz_buf[par] = jnp.minimum(pz_raw, mask_ref[diag.astype(jnp.int32)])

    def offset_body(state):
        d, _ = state
        lst_ref[0] = nq
        lst_ref[1] = nq

        def add(qi, n):
            lst_ref[2 + n] = qi
            return n + (done_ref[qi] == 0).astype(jnp.int32)

        n = lax.fori_loop(d, nq, add, 0)
        for t in range(4):
            lst_ref[2 + n + t] = nq

        def body(i, c):
            sub_step(2 * i, 0, d)
            sub_step(2 * i + 1, 1, d)
            return c

        lax.fori_loop(0, jnp.where(n > 0, (n + 3) // 2, 0), body, 0)
        return d + 1, n

    lax.while_loop(lambda st: (st[0] < nq) & (st[1] > 0), offset_body, (0, 1))

    for qi in range(nq):
        o_ref[qi * tile:(qi + 1) * tile, :] = acc_ref[qi].astype(o_ref.dtype)


def _sb_attention(qkv, *, batch, seq, d_model):
    tile = SB_TILE
    nq = seq // tile
    pairs = SB_HEADS // 2
    qkv3 = qkv.reshape(batch, seq, 3 * d_model)
    kern = functools.partial(_sb_kernel, tile=tile, nq=nq)
    out = pl.pallas_call(
        kern,
        grid=(batch, pairs),
        in_specs=[
            pl.BlockSpec((None, seq, LANES), lambda b, p: (b, 0, p)),
            pl.BlockSpec((None, seq, LANES), lambda b, p: (b, 0, pairs + p)),
            pl.BlockSpec((None, seq, LANES), lambda b, p: (b, 0, 2 * pairs + p)),
        ],
        out_specs=pl.BlockSpec((None, seq, LANES), lambda b, p: (b, 0, p)),
        out_shape=jax.ShapeDtypeStruct((batch, seq, d_model), BF16),
        scratch_shapes=[
            pltpu.VMEM((2, 2 * tile, tile), F32),
            pltpu.VMEM((2, 2 * tile, tile), F32),
            pltpu.VMEM((2, 2 * tile, tile), F32),
            pltpu.VMEM((tile, tile), BF16),
            pltpu.VMEM((nq + 1, tile, LANES), F32),
            pltpu.VMEM((nq + 1, 2 * tile, 1), F32),
            pltpu.SMEM((nq + 8,), jnp.int32),
            pltpu.SMEM((nq + 1,), jnp.int32),
        ],
        compiler_params=_params("parallel", "parallel"),
        name="sb_attention",
    )(qkv3, qkv3, qkv3)
    return out.reshape(batch * seq, d_model)


def _ret_kernel(q_ref, k_ref, v_ref, g_ref, cos_ref, sin_ref, intra_ref, cross_ref,
                kv_ref, cd_ref, o_ref, state_ref, *, k_scale, chunk, n_sub):
    @pl.when(pl.program_id(2) == 0)
    def _():
        state_ref[...] = jnp.zeros_like(state_ref)

    hd = cos_ref.shape[1]
    cross_d = cross_ref[...]
    kv_d = kv_ref[...]

    for c in range(n_sub):
        rs = slice(c * chunk, (c + 1) * chunk)
        cos = cos_ref[rs, :]
        sin = sin_ref[rs, :]

        def rope(x_ref):
            x = x_ref[rs, :].astype(F32)
            x1, x2 = x[:, :hd], x[:, hd:]
            return x1 * cos - x2 * sin, x1 * sin + x2 * cos

        q1, q2 = rope(q_ref)
        k1, k2 = rope(k_ref)
        k1, k2 = k1 * k_scale, k2 * k_scale
        v = v_ref[rs, :]

        qr = jnp.concatenate([q1, q2], axis=1).astype(BF16)
        kr = jnp.concatenate([k1, k2], axis=1).astype(BF16)
        scores = lax.dot_general(qr, kr, (((1,), (1,)), ((), ())),
                                 preferred_element_type=F32) * intra_ref[...]
        inner = jnp.dot(scores.astype(BF16), v, preferred_element_type=F32)

        state = state_ref[...]
        qc = jnp.concatenate([q1 * cross_d, q2 * cross_d], axis=1).astype(BF16)
        cross = jnp.dot(qc, state.astype(BF16), preferred_element_type=F32)

        kd = jnp.concatenate([k1 * kv_d, k2 * kv_d], axis=1).astype(BF16)
        state_ref[...] = state * cd_ref[...] + lax.dot_general(
            kd, v, (((0,), (0,)), ((), ())), preferred_element_type=F32)

        y = inner + cross
        y = y * lax.rsqrt(jnp.mean(y * y, axis=-1, keepdims=True) + EPS)
        g = g_ref[rs, :].astype(F32)
        o_ref[rs, :] = (g * jax.nn.sigmoid(g) * y).astype(o_ref.dtype)


def _retention_tables(seq, key_dim, val_dim, chunk):
    half = key_dim // 2
    inv_freq = ROPE_BASE ** (-jnp.arange(0, key_dim, 2, dtype=F32) / key_dim)
    ang = jnp.arange(seq).astype(F32)[:, None] * inv_freq[None, :]
    log_gamma = jnp.log1p(-jnp.exp2(-5.0 - jnp.arange(RET_HEADS, dtype=F32)))
    idx = jnp.arange(chunk, dtype=F32)
    diff = idx[:, None] - idx[None, :]
    intra = jnp.where(diff >= 0, jnp.exp(log_gamma[:, None, None] * jnp.maximum(diff, 0.0)), 0.0)
    cross = jnp.exp(log_gamma[:, None] * (idx + 1.0))
    kv = jnp.exp(log_gamma[:, None] * (chunk - 1.0 - idx))
    cd = jnp.exp(log_gamma * chunk)
    return dict(
        cos=jnp.cos(ang), sin=jnp.sin(ang), intra=intra,
        cross=jnp.broadcast_to(cross[:, :, None], (RET_HEADS, chunk, half)),
        kv=jnp.broadcast_to(kv[:, :, None], (RET_HEADS, chunk, half)),
        cd=jnp.broadcast_to(cd[:, None, None], (RET_HEADS, 1, val_dim)),
    )


def _retention(proj, *, batch, seq, d_model):
    chunk = RET_CHUNK
    key_dim = d_model // RET_HEADS
    val_dim = 2 * d_model // RET_HEADS
    half = key_dim // 2
    tb = _retention_tables(seq, key_dim, val_dim, chunk)
    proj3 = proj.reshape(batch, seq, 6 * d_model)
    nh = RET_HEADS
    n_sub = RET_CHUNKS_PER_STEP
    blk = chunk * n_sub
    kern = functools.partial(_ret_kernel, k_scale=key_dim ** -0.5, chunk=chunk, n_sub=n_sub)
    out = pl.pallas_call(
        kern,
        grid=(batch, nh, seq // blk),
        in_specs=[
            pl.BlockSpec((None, blk, key_dim), lambda b, h, c: (b, c, h)),
            pl.BlockSpec((None, blk, key_dim), lambda b, h, c: (b, c, nh + h)),
            pl.BlockSpec((None, blk, val_dim), lambda b, h, c: (b, c, nh + h)),
            pl.BlockSpec((None, blk, val_dim), lambda b, h, c: (b, c, 2 * nh + h)),
            pl.BlockSpec((blk, half), lambda b, h, c: (c, 0)),
            pl.BlockSpec((blk, half), lambda b, h, c: (c, 0)),
            pl.BlockSpec((None, chunk, chunk), lambda b, h, c: (h, 0, 0)),
            pl.BlockSpec((None, chunk, half), lambda b, h, c: (h, 0, 0)),
            pl.BlockSpec((None, chunk, half), lambda b, h, c: (h, 0, 0)),
            pl.BlockSpec((None, 1, val_dim), lambda b, h, c: (h, 0, 0)),
        ],
        out_specs=pl.BlockSpec((None, blk, val_dim), lambda b, h, c: (b, c, h)),
        out_shape=jax.ShapeDtypeStruct((batch, seq, nh * val_dim), BF16),
        scratch_shapes=[pltpu.VMEM((key_dim, val_dim), F32)],
        compiler_params=_params("parallel", "parallel", "arbitrary"),
        name="retention",
    )(proj3, proj3, proj3, proj3, tb["cos"], tb["sin"], tb["intra"], tb["cross"], tb["kv"], tb["cd"])
    return out.reshape(batch * seq, nh * val_dim)


def _tail_kernel(h_ref, mix_ref, p_ref, wo_ref, gm_ref, wu_ref, wd_ref, gp_ref, wg_ref, wp_ref,
                 fg_ref, o_ref, xn_ref, *, final_norm):
    j = pl.program_id(1)

    @pl.when(j == 0)
    def _():
        h1 = h_ref[...] + jnp.dot(mix_ref[...], wo_ref[...], preferred_element_type=F32)
        xn_ref[...] = _rms_norm(h1, gm_ref[...]).astype(BF16)
        o_ref[...] = h1

    a = jnp.maximum(jnp.dot(xn_ref[...], wu_ref[...], preferred_element_type=F32), 0.0)
    o_ref[...] += jnp.dot((a * a).astype(BF16), wd_ref[...], preferred_element_type=F32)

    @pl.when(j == pl.num_programs(1) - 1)
    def _():
        h2 = o_ref[...]
        xn = _rms_norm(h2, gp_ref[...]).astype(BF16)
        gate = jax.nn.sigmoid(jnp.dot(xn, wg_ref[...], preferred_element_type=F32))
        up = jnp.dot(p_ref[...].astype(BF16), wp_ref[...], preferred_element_type=F32)
        out = h2 + up * gate
        if final_norm:
            out = _rms_norm(out, fg_ref[...])
        o_ref[...] = out


def _tail(h, mix, p, layer, w_out, g_mlp, w_up, w_down, g_ple, w_gate, w_ple, final_g, *, final_norm):
    t, d = h.shape
    dm = mix.shape[1]
    dp = p.shape[2]
    f = w_up.shape[1]
    tm, tf = MLP_ROW_TILE, MLP_FF_TILE
    once = dict(pipeline_mode=pl.Buffered(1))
    return pl.pallas_call(
        functools.partial(_tail_kernel, final_norm=final_norm),
        grid=(t // tm, f // tf),
        in_specs=[
            pl.BlockSpec((tm, d), lambda i, j: (i, 0)),
            pl.BlockSpec((tm, dm), lambda i, j: (i, 0)),
            pl.BlockSpec((None, tm, dp), lambda i, j: (layer, i, 0)),
            pl.BlockSpec((dm, d), lambda i, j: (0, 0), **once),
            pl.BlockSpec((1, d), lambda i, j: (0, 0)),
            pl.BlockSpec((d, tf), lambda i, j: (0, j)),
            pl.BlockSpec((tf, d), lambda i, j: (j, 0)),
            pl.BlockSpec((1, d), lambda i, j: (0, 0)),
            pl.BlockSpec((d, d), lambda i, j: (0, 0), **once),
            pl.BlockSpec((dp, d), lambda i, j: (0, 0), **once),
            pl.BlockSpec((1, d), lambda i, j: (0, 0)),
        ],
        out_specs=pl.BlockSpec((tm, d), lambda i, j: (i, 0)),
        out_shape=jax.ShapeDtypeStruct((t, d), F32),
        scratch_shapes=[pltpu.VMEM((tm, d), BF16)],
        compiler_params=_params("parallel", "arbitrary"),
        name="tail",
    )(h, mix, p, w_out, g_mlp.reshape(1, d), w_up, w_down, g_ple.reshape(1, d), w_gate, w_ple,
      final_g.reshape(1, d))


def kernel(x, p, mix_norm, sb_w_in, sb_w_out, ret_w_in, ret_w_out, mlp_norm, mlp_w_up,
           mlp_w_down, ple_norm, ple_w_gate, ple_w_up, final_norm):
    batch, seq, d_model = x.shape
    depth = p.shape[0]
    t = batch * seq
    h = x.reshape(t, d_model)
    p2 = p.reshape(depth, t, p.shape[-1])
    for i in range(depth):
        if i % 2 == 0:
            q_scale = LOG2E * (d_model // SB_HEADS) ** -0.5
            col_scale = jnp.concatenate([jnp.full((d_model,), q_scale, F32), jnp.ones((2 * d_model,), F32)])
            qkv = _norm_proj(h, mix_norm[i], sb_w_in[i // 2].astype(BF16), col_scale)
            mix = _sb_attention(qkv, batch=batch, seq=seq, d_model=d_model)
            w_out = sb_w_out[i // 2]
        else:
            proj = _norm_proj(h, mix_norm[i], ret_w_in[i // 2].astype(BF16), jnp.ones((6 * d_model,), F32))
            mix = _retention(proj, batch=batch, seq=seq, d_model=d_model)
            w_out = ret_w_out[i // 2]
        h = _tail(h, mix, p2, i, w_out.astype(BF16), mlp_norm[i], mlp_w_up[i].astype(BF16),
                  mlp_w_down[i].astype(BF16), ple_norm[i], ple_w_gate[i].astype(BF16),
                  ple_w_up[i].astype(BF16), final_norm, final_norm=(i == depth - 1))
    return h.reshape(batch, seq, d_model)
```

```python
import functools
import math

import jax
import jax.numpy as jnp
from jax import lax
from jax.experimental import pallas as pl
from jax.experimental.pallas import tpu as pltpu

F32 = jnp.float32
BF16 = jnp.bfloat16

EPS = 1e-6
SB_HEADS = 16
RET_HEADS = 4
ROPE_BASE = 10000.0

VMEM_LIMIT_BYTES = 56 * 1024 * 1024
LANES = 128
LOG2E = math.log2(math.e)
MASK_BIG = 1e30
PZ_MAX = 126.0
SKIP_LOG2 = -160.0

PROJ_ROW_TILE = 512
PROJ_COL_TILE = 1536
TAIL_ROW_TILE = 512
MLP_FF_TILE = 1024
SB_TILE = 256
RET_CHUNK = 256
RET_CHUNKS_PER_STEP = 4


def _params(*sem):
    return pltpu.CompilerParams(dimension_semantics=sem, vmem_limit_bytes=VMEM_LIMIT_BYTES)


def _rms_norm(x, g):
    ms = jnp.mean(x * x, axis=-1, keepdims=True)
    return x * lax.rsqrt(ms + EPS) * g


def _norm_proj_kernel(x_ref, g_ref, w_ref, cs_ref, o_ref, *, tn):
    xn = _rms_norm(x_ref[...], g_ref[...]).astype(BF16)
    for c in range(w_ref.shape[1] // tn):
        cols = slice(c * tn, (c + 1) * tn)
        acc = jnp.dot(xn, w_ref[:, cols], preferred_element_type=F32)
        o_ref[:, cols] = (acc * cs_ref[:, cols]).astype(o_ref.dtype)


def _norm_proj(h, g, w, col_scale):
    t, d = h.shape
    n = w.shape[1]
    tm = PROJ_ROW_TILE

    def whole(shape):
        return pl.BlockSpec(shape, lambda i: (0, 0), pipeline_mode=pl.Buffered(1))

    return pl.pallas_call(
        functools.partial(_norm_proj_kernel, tn=PROJ_COL_TILE),
        grid=(t // tm,),
        in_specs=[
            pl.BlockSpec((tm, d), lambda i: (i, 0)),
            whole((1, d)),
            whole((d, n)),
            whole((1, n)),
        ],
        out_specs=pl.BlockSpec((tm, n), lambda i: (i, 0)),
        out_shape=jax.ShapeDtypeStruct((t, n), BF16),
        compiler_params=_params("parallel"),
        name="norm_proj",
    )(h, g.reshape(1, d), w, col_scale.reshape(1, n))


def _sb_kernel(q_ref, k_ref, v_ref, mask_ref, incl_ref, o_ref, pz_buf, sfx_buf, acc_ref,
               carry_ref, lst_q, lst_k, done_ref, *, tile, nq):
    half = LANES // 2
    lane = lax.broadcasted_iota(jnp.int32, (1, LANES), 1)
    head0 = lane < half

    @pl.when((pl.program_id(0) == 0) & (pl.program_id(1) == 0))
    def _():
        pz_buf[...] = jnp.zeros_like(pz_buf)
        sfx_buf[...] = jnp.zeros_like(sfx_buf)
        acc_ref[...] = jnp.zeros_like(acc_ref)
        carry_ref[...] = jnp.zeros_like(carry_ref)

    for i in range(nq + 1):
        done_ref[i] = 0

    def rows(idx):
        return pl.ds(pl.multiple_of(idx * tile, tile), tile)

    def entry(e):
        st = lst_q[e]
        kj = lst_k[e]
        qi = jnp.minimum(st, nq - 1)
        return st, qi, kj, (st == kj).astype(jnp.int32)

    def sub_step(e, par):
        st, _, kj, diag = entry(e)
        fresh = (diag == 1) | (st == nq)
        carry = jnp.where(fresh, 0.0, carry_ref[st])
        suffix = sfx_buf[par]
        a = jnp.exp2(suffix + pz_buf[par] + carry)
        carry = carry + suffix[:, 0:1]
        carry_ref[st] = carry
        done_ref[st] = (jnp.max(carry) < SKIP_LOG2).astype(jnp.int32)
        res = jnp.dot(a.astype(BF16), v_ref[rows(kj), :], preferred_element_type=F32)
        acc_ref[st] = (jnp.where(head0, res[:tile], res[tile:])
                       + jnp.where(fresh, 0.0, acc_ref[st]))

        pz = pz_buf[1 - par]
        l = jnp.log(1.0 + jnp.exp2(pz)) * LOG2E
        sfx_buf[1 - par] = jnp.dot(l.astype(BF16), incl_ref[...], preferred_element_type=F32)

        _, qi, kj, diag = entry(e + 2)
        q = q_ref[rows(qi), :]
        qzero = jnp.zeros_like(q)
        q2 = jnp.concatenate([jnp.where(head0, q, qzero), jnp.where(head0, qzero, q)], axis=0)
        pz_raw = lax.dot_general(q2, k_ref[rows(kj), :], (((1,), (1,)), ((), ())),
                                 preferred_element_type=F32)
        pz_buf[par] = jnp.minimum(pz_raw, mask_ref[diag])

    def put(i, st, kj):
        lst_q[i] = st
        lst_k[i] = kj

    def offset_body(state):
        d, _ = state
        put(0, nq, 0)
        put(1, nq, 0)
        n = 0
        for off in range(2):
            def add(qi, n, off=off):
                put(2 + n, qi, qi - d - off)
                return n + (done_ref[qi] == 0).astype(jnp.int32)

            n = lax.fori_loop(d + off, nq, add, n)
        for t in range(4):
            put(2 + n + t, nq, 0)

        def body(i, c):
            sub_step(2 * i, 0)
            sub_step(2 * i + 1, 1)
            return c

        lax.fori_loop(0, jnp.where(n > 0, (n + 3) // 2, 0), body, 0)
        return d + 2, n

    lax.while_loop(lambda st: (st[0] < nq) & (st[1] > 0), offset_body, (0, 1))

    for qi in range(nq):
        o_ref[qi * tile:(qi + 1) * tile, :] = acc_ref[qi].astype(o_ref.dtype)


def _sb_constants(tile):
    row = lax.broadcasted_iota(jnp.int32, (tile, tile), 0)
    col = lax.broadcasted_iota(jnp.int32, (tile, tile), 1)
    diag = jnp.where(col >= row, -MASK_BIG, PZ_MAX).astype(F32)
    diag = jnp.concatenate([diag, diag], axis=0)
    mask = jnp.stack([jnp.full_like(diag, PZ_MAX), diag])
    incl = jnp.where(row >= col, -1.0, 0.0).astype(BF16)
    return mask, incl


def _sb_attention(qkv, *, batch, seq, d_model):
    tile = SB_TILE
    nq = seq // tile
    pairs = SB_HEADS // 2
    qkv3 = qkv.reshape(batch, seq, 3 * d_model)
    mask, incl = _sb_constants(tile)
    kern = functools.partial(_sb_kernel, tile=tile, nq=nq)
    out = pl.pallas_call(
        kern,
        grid=(batch, pairs),
        in_specs=[
            pl.BlockSpec((None, seq, LANES), lambda b, p: (b, 0, p)),
            pl.BlockSpec((None, seq, LANES), lambda b, p: (b, 0, pairs + p)),
            pl.BlockSpec((None, seq, LANES), lambda b, p: (b, 0, 2 * pairs + p)),
            pl.BlockSpec((2, 2 * tile, tile), lambda b, p: (0, 0, 0), pipeline_mode=pl.Buffered(1)),
            pl.BlockSpec((tile, tile), lambda b, p: (0, 0), pipeline_mode=pl.Buffered(1)),
        ],
        out_specs=pl.BlockSpec((None, seq, LANES), lambda b, p: (b, 0, p)),
        out_shape=jax.ShapeDtypeStruct((batch, seq, d_model), BF16),
        scratch_shapes=[
            pltpu.VMEM((2, 2 * tile, tile), F32),
            pltpu.VMEM((2, 2 * tile, tile), F32),
            pltpu.VMEM((nq + 1, tile, LANES), F32),
            pltpu.VMEM((nq + 1, 2 * tile, 1), F32),
            pltpu.SMEM((2 * nq + 8,), jnp.int32),
            pltpu.SMEM((2 * nq + 8,), jnp.int32),
            pltpu.SMEM((nq + 1,), jnp.int32),
        ],
        compiler_params=_params("arbitrary", "arbitrary"),
        name="sb_attention",
    )(qkv3, qkv3, qkv3, mask, incl)
    return out.reshape(batch * seq, d_model)


def _ret_kernel(q_ref, k_ref, v_ref, g_ref, cos_ref, sin_ref, intra_ref, cross_ref,
                kv_ref, cd_ref, o_ref, state_ref, *, k_scale, chunk, n_sub):
    @pl.when(pl.program_id(2) == 0)
    def _():
        state_ref[...] = jnp.zeros_like(state_ref)

    hd = cos_ref.shape[1]
    cross_d = cross_ref[...]
    kv_d = kv_ref[...]

    for c in range(n_sub):
        rs = slice(c * chunk, (c + 1) * chunk)
        cos = cos_ref[rs, :]
        sin = sin_ref[rs, :]

        def rope(x_ref):
            x = x_ref[rs, :].astype(F32)
            x1, x2 = x[:, :hd], x[:, hd:]
            return x1 * cos - x2 * sin, x1 * sin + x2 * cos

        q1, q2 = rope(q_ref)
        k1, k2 = rope(k_ref)
        k1, k2 = k1 * k_scale, k2 * k_scale
        v = v_ref[rs, :]

        qr = jnp.concatenate([q1, q2], axis=1).astype(BF16)
        kr = jnp.concatenate([k1, k2], axis=1).astype(BF16)
        scores = lax.dot_general(qr, kr, (((1,), (1,)), ((), ())),
                                 preferred_element_type=F32) * intra_ref[...]
        inner = jnp.dot(scores.astype(BF16), v, preferred_element_type=F32)

        state = state_ref[...]
        qc = jnp.concatenate([q1 * cross_d, q2 * cross_d], axis=1).astype(BF16)
        cross = jnp.dot(qc, state.astype(BF16), preferred_element_type=F32)

        kd = jnp.concatenate([k1 * kv_d, k2 * kv_d], axis=1).astype(BF16)
        state_ref[...] = state * cd_ref[...] + lax.dot_general(
            kd, v, (((0,), (0,)), ((), ())), preferred_element_type=F32)

        y = inner + cross
        y = y * lax.rsqrt(jnp.mean(y * y, axis=-1, keepdims=True) + EPS)
        g = g_ref[rs, :].astype(F32)
        o_ref[rs, :] = (g * jax.nn.sigmoid(g) * y).astype(o_ref.dtype)


def _retention_tables(seq, key_dim, val_dim, chunk):
    half = key_dim // 2
    inv_freq = ROPE_BASE ** (-jnp.arange(0, key_dim, 2, dtype=F32) / key_dim)
    ang = jnp.arange(seq).astype(F32)[:, None] * inv_freq[None, :]
    log_gamma = jnp.log1p(-jnp.exp2(-5.0 - jnp.arange(RET_HEADS, dtype=F32)))
    idx = jnp.arange(chunk, dtype=F32)
    diff = idx[:, None] - idx[None, :]
    intra = jnp.where(diff >= 0, jnp.exp(log_gamma[:, None, None] * jnp.maximum(diff, 0.0)), 0.0)
    cross = jnp.exp(log_gamma[:, None] * (idx + 1.0))
    kv = jnp.exp(log_gamma[:, None] * (chunk - 1.0 - idx))
    cd = jnp.exp(log_gamma * chunk)
    return dict(
        cos=jnp.cos(ang), sin=jnp.sin(ang), intra=intra,
        cross=jnp.broadcast_to(cross[:, :, None], (RET_HEADS, chunk, half)),
        kv=jnp.broadcast_to(kv[:, :, None], (RET_HEADS, chunk, half)),
        cd=jnp.broadcast_to(cd[:, None, None], (RET_HEADS, 1, val_dim)),
    )


def _retention(proj, *, batch, seq, d_model):
    chunk = RET_CHUNK
    key_dim = d_model // RET_HEADS
    val_dim = 2 * d_model // RET_HEADS
    half = key_dim // 2
    tb = _retention_tables(seq, key_dim, val_dim, chunk)
    proj3 = proj.reshape(batch, seq, 6 * d_model)
    nh = RET_HEADS
    n_sub = RET_CHUNKS_PER_STEP
    blk = chunk * n_sub
    kern = functools.partial(_ret_kernel, k_scale=key_dim ** -0.5, chunk=chunk, n_sub=n_sub)
    out = pl.pallas_call(
        kern,
        grid=(batch, nh, seq // blk),
        in_specs=[
            pl.BlockSpec((None, blk, key_dim), lambda b, h, c: (b, c, h)),
            pl.BlockSpec((None, blk, key_dim), lambda b, h, c: (b, c, nh + h)),
            pl.BlockSpec((None, blk, val_dim), lambda b, h, c: (b, c, nh + h)),
            pl.BlockSpec((None, blk, val_dim), lambda b, h, c: (b, c, 2 * nh + h)),
            pl.BlockSpec((blk, half), lambda b, h, c: (c, 0)),
            pl.BlockSpec((blk, half), lambda b, h, c: (c, 0)),
            pl.BlockSpec((None, chunk, chunk), lambda b, h, c: (h, 0, 0)),
            pl.BlockSpec((None, chunk, half), lambda b, h, c: (h, 0, 0)),
            pl.BlockSpec((None, chunk, half), lambda b, h, c: (h, 0, 0)),
            pl.BlockSpec((None, 1, val_dim), lambda b, h, c: (h, 0, 0)),
        ],
        out_specs=pl.BlockSpec((None, blk, val_dim), lambda b, h, c: (b, c, h)),
        out_shape=jax.ShapeDtypeStruct((batch, seq, nh * val_dim), BF16),
        scratch_shapes=[pltpu.VMEM((key_dim, val_dim), F32)],
        compiler_params=_params("parallel", "parallel", "arbitrary"),
        name="retention",
    )(proj3, proj3, proj3, proj3, tb["cos"], tb["sin"], tb["intra"], tb["cross"], tb["kv"], tb["cd"])
    return out.reshape(batch * seq, nh * val_dim)


def _tail_kernel(h_ref, mix_ref, p_ref, wo_ref, gm_ref, wu_ref, wd_ref, gp_ref, wg_ref, wp_ref,
                 fg_ref, o_ref, *, final_norm, tf):
    h1 = h_ref[...] + jnp.dot(mix_ref[...], wo_ref[...], preferred_element_type=F32)
    xn = _rms_norm(h1, gm_ref[...]).astype(BF16)
    h2 = h1
    for c in range(wu_ref.shape[1] // tf):
        a = jnp.maximum(jnp.dot(xn, wu_ref[:, c * tf:(c + 1) * tf], preferred_element_type=F32), 0.0)
        h2 = h2 + jnp.dot((a * a).astype(BF16), wd_ref[c * tf:(c + 1) * tf, :],
                          preferred_element_type=F32)
    xn = _rms_norm(h2, gp_ref[...]).astype(BF16)
    gate = jax.nn.sigmoid(jnp.dot(xn, wg_ref[...], preferred_element_type=F32))
    up = jnp.dot(p_ref[...].astype(BF16), wp_ref[...], preferred_element_type=F32)
    out = h2 + up * gate
    if final_norm:
        out = _rms_norm(out, fg_ref[...])
    o_ref[...] = out


def _tail(h, mix, p, layer, w_out, g_mlp, w_up, w_down, g_ple, w_gate, w_ple, final_g, *, final_norm):
    t, d = h.shape
    dm = mix.shape[1]
    dp = p.shape[2]
    f = w_up.shape[1]
    tm = TAIL_ROW_TILE

    def whole(shape):
        return pl.BlockSpec(shape, lambda i: (0, 0), pipeline_mode=pl.Buffered(1))

    return pl.pallas_call(
        functools.partial(_tail_kernel, final_norm=final_norm, tf=MLP_FF_TILE),
        grid=(t // tm,),
        in_specs=[
            pl.BlockSpec((tm, d), lambda i: (i, 0)),
            pl.BlockSpec((tm, dm), lambda i: (i, 0)),
            pl.BlockSpec((None, tm, dp), lambda i: (layer, i, 0)),
            whole((dm, d)),
            whole((1, d)),
            whole((d, f)),
            whole((f, d)),
            whole((1, d)),
            whole((d, d)),
            whole((dp, d)),
            whole((1, d)),
        ],
        out_specs=pl.BlockSpec((tm, d), lambda i: (i, 0)),
        out_shape=jax.ShapeDtypeStruct((t, d), F32),
        compiler_params=_params("parallel"),
        name="tail",
    )(h, mix, p, w_out, g_mlp.reshape(1, d), w_up, w_down, g_ple.reshape(1, d), w_gate, w_ple,
      final_g.reshape(1, d))


def kernel(x, p, mix_norm, sb_w_in, sb_w_out, ret_w_in, ret_w_out, mlp_norm, mlp_w_up,
           mlp_w_down, ple_norm, ple_w_gate, ple_w_up, final_norm):
    batch, seq, d_model = x.shape
    depth = p.shape[0]
    t = batch * seq
    h = x.reshape(t, d_model)
    p2 = p.reshape(depth, t, p.shape[-1])
    for i in range(depth):
        if i % 2 == 0:
            q_scale = LOG2E * (d_model // SB_HEADS) ** -0.5
            col_scale = jnp.concatenate([jnp.full((d_model,), q_scale, F32), jnp.ones((2 * d_model,), F32)])
            qkv = _norm_proj(h, mix_norm[i], sb_w_in[i // 2].astype(BF16), col_scale)
            mix = _sb_attention(qkv, batch=batch, seq=seq, d_model=d_model)
            w_out = sb_w_out[i // 2]
        else:
            proj = _norm_proj(h, mix_norm[i], ret_w_in[i // 2].astype(BF16), jnp.ones((6 * d_model,), F32))
            mix = _retention(proj, batch=batch, seq=seq, d_model=d_model)
            w_out = ret_w_out[i // 2]
        h = _tail(h, mix, p2, i, w_out.astype(BF16), mlp_norm[i], mlp_w_up[i].astype(BF16),
                  mlp_w_down[i].astype(BF16), ple_norm[i], ple_w_gate[i].astype(BF16),
                  ple_w_up[i].astype(BF16), final_norm, final_norm=(i == depth - 1))
    return h.reshape(batch, seq, d_model)
```

```python
import functools
import math

import jax
import jax.numpy as jnp
import numpy as np
from jax import lax
from jax.experimental import pallas as pl
from jax.experimental.pallas import tpu as pltpu

F32 = jnp.float32
BF16 = jnp.bfloat16

EPS = 1e-6
SB_HEADS = 16
RET_HEADS = 4
ROPE_BASE = 10000.0

VMEM_LIMIT_BYTES = 56 * 1024 * 1024
LANES = 128
LOG2E = math.log2(math.e)
MASK_BIG = 1e30
PZ_MAX = 126.0
SKIP_LOG2 = -160.0

PROJ_ROW_TILE = 512
PROJ_COL_TILE = 1536
TAIL_ROW_TILE = 512
MLP_FF_TILE = 1024
SB_TILE = 256
RET_CHUNK = 256
RET_CHUNKS_PER_STEP = 4


def _params(*sem):
    return pltpu.CompilerParams(dimension_semantics=sem, vmem_limit_bytes=VMEM_LIMIT_BYTES)


def _rms_norm(x, g):
    ms = jnp.mean(x * x, axis=-1, keepdims=True)
    return x * lax.rsqrt(ms + EPS) * g


def _norm_proj_kernel(x_ref, g_ref, w_ref, cs_ref, o_ref, *, tn):
    xn = _rms_norm(x_ref[...], g_ref[...]).astype(BF16)
    for c in range(w_ref.shape[1] // tn):
        cols = slice(c * tn, (c + 1) * tn)
        acc = jnp.dot(xn, w_ref[:, cols], preferred_element_type=F32)
        o_ref[:, cols] = (acc * cs_ref[:, cols]).astype(o_ref.dtype)


def _norm_proj(h, g, w, col_scale):
    t, d = h.shape
    n = w.shape[1]
    tm = PROJ_ROW_TILE

    def whole(shape):
        return pl.BlockSpec(shape, lambda i: (0, 0), pipeline_mode=pl.Buffered(1))

    return pl.pallas_call(
        functools.partial(_norm_proj_kernel, tn=PROJ_COL_TILE),
        grid=(t // tm,),
        in_specs=[
            pl.BlockSpec((tm, d), lambda i: (i, 0)),
            whole((1, d)),
            whole((d, n)),
            whole((1, n)),
        ],
        out_specs=pl.BlockSpec((tm, n), lambda i: (i, 0)),
        out_shape=jax.ShapeDtypeStruct((t, n), BF16),
        compiler_params=_params("parallel"),
        name="norm_proj",
    )(h, g.reshape(1, d), w, col_scale.reshape(1, n))


def _sb_kernel(q_ref, k_ref, v_ref, mask_ref, incl_ref, o_ref, pz_buf, sfx_buf, acc_ref,
               carry_ref, lst_q, lst_k, done_ref, *, tile, nq):
    half = LANES // 2
    lane = lax.broadcasted_iota(jnp.int32, (1, LANES), 1)
    head0 = lane < half

    @pl.when((pl.program_id(0) == 0) & (pl.program_id(1) == 0))
    def _():
        pz_buf[...] = jnp.zeros_like(pz_buf)
        sfx_buf[...] = jnp.zeros_like(sfx_buf)
        acc_ref[...] = jnp.zeros_like(acc_ref)
        carry_ref[...] = jnp.zeros_like(carry_ref)

    for i in range(nq + 1):
        done_ref[i] = 0

    def rows(idx):
        if isinstance(idx, int):
            return slice(idx * tile, (idx + 1) * tile)
        return pl.ds(pl.multiple_of(idx * tile, tile), tile)


    def stage_z(st, kj, slot, fresh, flag):
        suffix = sfx_buf[slot]
        if fresh is True:
            carry = suffix[:, 0:1]
            a = jnp.exp2(suffix + pz_buf[slot])
        else:
            carry = carry_ref[st] if fresh is False else jnp.where(fresh, 0.0, carry_ref[st])
            a = jnp.exp2(suffix + pz_buf[slot] + carry)
            carry = carry + suffix[:, 0:1]
        carry_ref[st] = carry
        if flag:
            done_ref[st] = (jnp.max(carry) < SKIP_LOG2).astype(jnp.int32)
        res = jnp.dot(a.astype(BF16), v_ref[rows(kj), :], preferred_element_type=F32)
        res = jnp.where(head0, res[:tile], res[tile:])
        if fresh is True:
            acc_ref[st] = res
        elif fresh is False:
            acc_ref[st] = res + acc_ref[st]
        else:
            acc_ref[st] = res + jnp.where(fresh, 0.0, acc_ref[st])

    def stage_y(slot):
        l = jnp.log(1.0 + jnp.exp2(pz_buf[slot])) * LOG2E
        sfx_buf[slot] = jnp.dot(l.astype(BF16), incl_ref[...], preferred_element_type=F32)

    def stage_x(qi, kj, slot, diag):
        q = q_ref[rows(qi), :]
        qzero = jnp.zeros_like(q)
        q2 = jnp.concatenate([jnp.where(head0, q, qzero), jnp.where(head0, qzero, q)], axis=0)
        pz_raw = lax.dot_general(q2, k_ref[rows(kj), :], (((1,), (1,)), ((), ())),
                                 preferred_element_type=F32)
        pz_buf[slot] = jnp.minimum(pz_raw, mask_ref[diag])

    first = [(qi, qi) for qi in range(nq)] + [(qi, qi - 1) for qi in range(1, nq)]
    for s in range(len(first) + 2):
        if s >= 2:
            qi, kj = first[s - 2]
            stage_z(qi, kj, s % 2, qi == kj, qi != kj)
        if 1 <= s <= len(first):
            stage_y((s - 1) % 2)
        if s < len(first):
            qi, kj = first[s]
            stage_x(qi, kj, s % 2, int(qi == kj))

    def entry(e):
        st = lst_q[e]
        return st, jnp.minimum(st, nq - 1), lst_k[e]

    def sub_step(e, slot):
        st, _, kj = entry(e)
        stage_z(st, kj, slot, st == nq, True)
        stage_y(1 - slot)
        _, qi, kj = entry(e + 2)
        stage_x(qi, kj, slot, 0)

    def put(i, st, kj):
        lst_q[i] = st
        lst_k[i] = kj

    def offset_body(state):
        d, _ = state
        put(0, nq, 0)
        put(1, nq, 0)
        n = 0
        for off in range(2):
            def add(qi, n, off=off):
                put(2 + n, qi, qi - d - off)
                return n + (done_ref[qi] == 0).astype(jnp.int32)

            n = lax.fori_loop(d + off, nq, add, n)
        for t in range(4):
            put(2 + n + t, nq, 0)

        def body(i, c):
            sub_step(2 * i, 0)
            sub_step(2 * i + 1, 1)
            return c

        lax.fori_loop(0, jnp.where(n > 0, (n + 3) // 2, 0), body, 0)
        return d + 2, n

    lax.while_loop(lambda st: (st[0] < nq) & (st[1] > 0), offset_body, (2, 1))

    for qi in range(nq):
        o_ref[qi * tile:(qi + 1) * tile, :] = acc_ref[qi].astype(o_ref.dtype)


def _sb_constants(tile):
    row = np.arange(tile)[:, None]
    col = np.arange(tile)[None, :]
    diag = np.where(col >= row, -MASK_BIG, PZ_MAX).astype(np.float32)
    diag = np.concatenate([diag, diag], axis=0)
    mask = np.stack([np.full_like(diag, PZ_MAX), diag])
    incl = np.where(row >= col, -1.0, 0.0).astype(np.float32)
    return jnp.asarray(mask), jnp.asarray(incl, dtype=BF16)


def _sb_attention(qkv, *, batch, seq, d_model):
    tile = SB_TILE
    nq = seq // tile
    pairs = SB_HEADS // 2
    qkv3 = qkv.reshape(batch, seq, 3 * d_model)
    mask, incl = _sb_constants(tile)
    kern = functools.partial(_sb_kernel, tile=tile, nq=nq)
    out = pl.pallas_call(
        kern,
        grid=(batch, pairs),
        in_specs=[
            pl.BlockSpec((None, seq, LANES), lambda b, p: (b, 0, p)),
            pl.BlockSpec((None, seq, LANES), lambda b, p: (b, 0, pairs + p)),
            pl.BlockSpec((None, seq, LANES), lambda b, p: (b, 0, 2 * pairs + p)),
            pl.BlockSpec((2, 2 * tile, tile), lambda b, p: (0, 0, 0), pipeline_mode=pl.Buffered(1)),
            pl.BlockSpec((tile, tile), lambda b, p: (0, 0), pipeline_mode=pl.Buffered(1)),
        ],
        out_specs=pl.BlockSpec((None, seq, LANES), lambda b, p: (b, 0, p)),
        out_shape=jax.ShapeDtypeStruct((batch, seq, d_model), BF16),
        scratch_shapes=[
            pltpu.VMEM((2, 2 * tile, tile), F32),
            pltpu.VMEM((2, 2 * tile, tile), F32),
            pltpu.VMEM((nq + 1, tile, LANES), F32),
            pltpu.VMEM((nq + 1, 2 * tile, 1), F32),
            pltpu.SMEM((2 * nq + 8,), jnp.int32),
            pltpu.SMEM((2 * nq + 8,), jnp.int32),
            pltpu.SMEM((nq + 1,), jnp.int32),
        ],
        compiler_params=_params("arbitrary", "arbitrary"),
        name="sb_attention",
    )(qkv3, qkv3, qkv3, mask, incl)
    return out.reshape(batch * seq, d_model)


def _ret_kernel(q_ref, k_ref, v_ref, g_ref, cos_ref, sin_ref, intra_ref, cross_ref,
                kv_ref, cd_ref, o_ref, state_ref, *, chunk, n_sub):
    @pl.when(pl.program_id(2) == 0)
    def _():
        state_ref[...] = jnp.zeros_like(state_ref)

    hd = cos_ref.shape[1]
    cross_d = cross_ref[...]
    kv_d = kv_ref[...]

    for c in range(n_sub):
        rs = slice(c * chunk, (c + 1) * chunk)
        cos = cos_ref[rs, :]
        sin = sin_ref[rs, :]

        def rope(x_ref):
            x = x_ref[rs, :].astype(F32)
            x1, x2 = x[:, :hd], x[:, hd:]
            return x1 * cos - x2 * sin, x1 * sin + x2 * cos

        q1, q2 = rope(q_ref)
        k1, k2 = rope(k_ref)
        v = v_ref[rs, :]

        qr = jnp.concatenate([q1, q2], axis=1).astype(BF16)
        kr = jnp.concatenate([k1, k2], axis=1).astype(BF16)
        scores = lax.dot_general(qr, kr, (((1,), (1,)), ((), ())),
                                 preferred_element_type=F32) * intra_ref[...]
        inner = jnp.dot(scores.astype(BF16), v, preferred_element_type=F32)

        state = state_ref[...]
        qc = jnp.concatenate([q1 * cross_d, q2 * cross_d], axis=1).astype(BF16)
        cross = jnp.dot(qc, state.astype(BF16), preferred_element_type=F32)

        kd = jnp.concatenate([k1 * kv_d, k2 * kv_d], axis=1).astype(BF16)
        state_ref[...] = state * cd_ref[...] + lax.dot_general(
            kd, v, (((0,), (0,)), ((), ())), preferred_element_type=F32)

        y = inner + cross
        y = y * lax.rsqrt(jnp.mean(y * y, axis=-1, keepdims=True) + EPS)
        hg = 0.5 * g_ref[rs, :].astype(F32)
        o_ref[rs, :] = ((hg + hg * jnp.tanh(hg)) * y).astype(o_ref.dtype)


def _retention_tables(seq, key_dim, val_dim, chunk):
    half = key_dim // 2
    k_scale = key_dim ** -0.5
    inv_freq = ROPE_BASE ** (-np.arange(0, key_dim, 2, dtype=np.float64) / key_dim)
    ang = np.arange(seq, dtype=np.float64)[:, None] * inv_freq[None, :]
    log_gamma = np.log1p(-np.exp2(-5.0 - np.arange(RET_HEADS, dtype=np.float64)))
    idx = np.arange(chunk, dtype=np.float64)
    diff = idx[:, None] - idx[None, :]
    intra = np.where(diff >= 0, np.exp(log_gamma[:, None, None] * np.maximum(diff, 0.0)), 0.0)
    cross = np.exp(log_gamma[:, None] * (idx + 1.0))
    kv = np.exp(log_gamma[:, None] * (chunk - 1.0 - idx))
    cd = np.exp(log_gamma * chunk)
    tables = dict(
        cos=np.cos(ang), sin=np.sin(ang), intra=intra * k_scale,
        cross=np.broadcast_to(cross[:, :, None], (RET_HEADS, chunk, half)),
        kv=np.broadcast_to((kv * k_scale)[:, :, None], (RET_HEADS, chunk, half)),
        cd=np.broadcast_to(cd[:, None, None], (RET_HEADS, 1, val_dim)),
    )
    return {name: jnp.asarray(t, dtype=F32) for name, t in tables.items()}


def _retention(proj, *, batch, seq, d_model):
    chunk = RET_CHUNK
    key_dim = d_model // RET_HEADS
    val_dim = 2 * d_model // RET_HEADS
    half = key_dim // 2
    tb = _retention_tables(seq, key_dim, val_dim, chunk)
    proj3 = proj.reshape(batch, seq, 6 * d_model)
    nh = RET_HEADS
    n_sub = RET_CHUNKS_PER_STEP
    blk = chunk * n_sub
    kern = functools.partial(_ret_kernel, chunk=chunk, n_sub=n_sub)
    out = pl.pallas_call(
        kern,
        grid=(batch, nh, seq // blk),
        in_specs=[
            pl.BlockSpec((None, blk, key_dim), lambda b, h, c: (b, c, h)),
            pl.BlockSpec((None, blk, key_dim), lambda b, h, c: (b, c, nh + h)),
            pl.BlockSpec((None, blk, val_dim), lambda b, h, c: (b, c, nh + h)),
            pl.BlockSpec((None, blk, val_dim), lambda b, h, c: (b, c, 2 * nh + h)),
            pl.BlockSpec((blk, half), lambda b, h, c: (c, 0)),
            pl.BlockSpec((blk, half), lambda b, h, c: (c, 0)),
            pl.BlockSpec((None, chunk, chunk), lambda b, h, c: (h, 0, 0)),
            pl.BlockSpec((None, chunk, half), lambda b, h, c: (h, 0, 0)),
            pl.BlockSpec((None, chunk, half), lambda b, h, c: (h, 0, 0)),
            pl.BlockSpec((None, 1, val_dim), lambda b, h, c: (h, 0, 0)),
        ],
        out_specs=pl.BlockSpec((None, blk, val_dim), lambda b, h, c: (b, c, h)),
        out_shape=jax.ShapeDtypeStruct((batch, seq, nh * val_dim), BF16),
        scratch_shapes=[pltpu.VMEM((key_dim, val_dim), F32)],
        compiler_params=_params("parallel", "parallel", "arbitrary"),
        name="retention",
    )(proj3, proj3, proj3, proj3, tb["cos"], tb["sin"], tb["intra"], tb["cross"], tb["kv"], tb["cd"])
    return out.reshape(batch * seq, nh * val_dim)


def _tail_kernel(h_ref, mix_ref, p_ref, wo_ref, gm_ref, wu_ref, wd_ref, gp_ref, wg_ref, wp_ref,
                 fg_ref, o_ref, *, final_norm, tf):
    h1 = h_ref[...] + jnp.dot(mix_ref[...], wo_ref[...], preferred_element_type=F32)
    xn = _rms_norm(h1, gm_ref[...]).astype(BF16)
    h2 = h1
    for c in range(wu_ref.shape[1] // tf):
        a = jnp.maximum(jnp.dot(xn, wu_ref[:, c * tf:(c + 1) * tf], preferred_element_type=F32), 0.0)
        h2 = h2 + jnp.dot((a * a).astype(BF16), wd_ref[c * tf:(c + 1) * tf, :],
                          preferred_element_type=F32)
    xn = _rms_norm(h2, gp_ref[...]).astype(BF16)
    gate = jax.nn.sigmoid(jnp.dot(xn, wg_ref[...], preferred_element_type=F32))
    up = jnp.dot(p_ref[...].astype(BF16), wp_ref[...], preferred_element_type=F32)
    out = h2 + up * gate
    if final_norm:
        out = _rms_norm(out, fg_ref[...])
    o_ref[...] = out


def _tail(h, mix, p, layer, w_out, g_mlp, w_up, w_down, g_ple, w_gate, w_ple, final_g, *, final_norm):
    t, d = h.shape
    dm = mix.shape[1]
    dp = p.shape[2]
    f = w_up.shape[1]
    tm = TAIL_ROW_TILE

    def whole(shape):
        return pl.BlockSpec(shape, lambda i: (0, 0), pipeline_mode=pl.Buffered(1))

    return pl.pallas_call(
        functools.partial(_tail_kernel, final_norm=final_norm, tf=MLP_FF_TILE),
        grid=(t // tm,),
        in_specs=[
            pl.BlockSpec((tm, d), lambda i: (i, 0)),
            pl.BlockSpec((tm, dm), lambda i: (i, 0)),
            pl.BlockSpec((None, tm, dp), lambda i: (layer, i, 0)),
            whole((dm, d)),
            whole((1, d)),
            whole((d, f)),
            whole((f, d)),
            whole((1, d)),
            whole((d, d)),
            whole((dp, d)),
            whole((1, d)),
        ],
        out_specs=pl.BlockSpec((tm, d), lambda i: (i, 0)),
        out_shape=jax.ShapeDtypeStruct((t, d), F32),
        compiler_params=_params("parallel"),
        name="tail",
    )(h, mix, p, w_out, g_mlp.reshape(1, d), w_up, w_down, g_ple.reshape(1, d), w_gate, w_ple,
      final_g.reshape(1, d))


def kernel(x, p, mix_norm, sb_w_in, sb_w_out, ret_w_in, ret_w_out, mlp_norm, mlp_w_up,
           mlp_w_down, ple_norm, ple_w_gate, ple_w_up, final_norm):
    batch, seq, d_model = x.shape
    depth = p.shape[0]
    t = batch * seq
    h = x.reshape(t, d_model)
    p2 = p.reshape(depth, t, p.shape[-1])
    for i in range(depth):
        if i % 2 == 0:
            q_scale = LOG2E * (d_model // SB_HEADS) ** -0.5
            col_scale = jnp.concatenate([jnp.full((d_model,), q_scale, F32), jnp.ones((2 * d_model,), F32)])
            qkv = _norm_proj(h, mix_norm[i], sb_w_in[i // 2].astype(BF16), col_scale)
            mix = _sb_attention(qkv, batch=batch, seq=seq, d_model=d_model)
            w_out = sb_w_out[i // 2]
        else:
            proj = _norm_proj(h, mix_norm[i], ret_w_in[i // 2].astype(BF16), jnp.ones((6 * d_model,), F32))
            mix = _retention(proj, batch=batch, seq=seq, d_model=d_model)
            w_out = ret_w_out[i // 2]
        h = _tail(h, mix, p2, i, w_out.astype(BF16), mlp_norm[i], mlp_w_up[i].astype(BF16),
                  mlp_w_down[i].astype(BF16), ple_norm[i], ple_w_gate[i].astype(BF16),
                  ple_w_up[i].astype(BF16), final_norm, final_norm=(i == depth - 1))
    return h.reshape(batch, seq, d_model)
```

```python
import functools
import math

import jax
import jax.numpy as jnp
import numpy as np
from jax import lax
from jax.experimental import pallas as pl
from jax.experimental.pallas import tpu as pltpu

F32 = jnp.float32
BF16 = jnp.bfloat16

EPS = 1e-6
SB_HEADS = 16
RET_HEADS = 4
ROPE_BASE = 10000.0

VMEM_LIMIT_BYTES = 56 * 1024 * 1024
LANES = 128
LOG2E = math.log2(math.e)
MASK_BIG = 1e30
PZ_MAX = 126.0
SKIP_LOG2 = -160.0

PROJ_ROW_TILE = 1024
PROJ_COL_TILE = 1536
TAIL_ROW_TILE = 512
MLP_FF_TILE = 1024
SB_TILE = 256
RET_CHUNK = 256
RET_CHUNKS_PER_STEP = 16


def _params(*sem):
    return pltpu.CompilerParams(dimension_semantics=sem, vmem_limit_bytes=VMEM_LIMIT_BYTES)


def _rms_norm(x, g):
    ms = jnp.mean(x * x, axis=-1, keepdims=True)
    return x * lax.rsqrt(ms + EPS) * g


def _norm_proj_kernel(x_ref, g_ref, w_ref, cs_ref, o_ref, *, tn):
    xn = _rms_norm(x_ref[...], g_ref[...]).astype(BF16)
    for c in range(w_ref.shape[1] // tn):
        cols = slice(c * tn, (c + 1) * tn)
        acc = jnp.dot(xn, w_ref[:, cols], preferred_element_type=F32)
        o_ref[:, cols] = (acc * cs_ref[:, cols]).astype(o_ref.dtype)


def _norm_proj(h, g, w, col_scale):
    t, d = h.shape
    n = w.shape[1]
    tm = PROJ_ROW_TILE

    def whole(shape):
        return pl.BlockSpec(shape, lambda i: (0, 0), pipeline_mode=pl.Buffered(1))

    return pl.pallas_call(
        functools.partial(_norm_proj_kernel, tn=PROJ_COL_TILE),
        grid=(t // tm,),
        in_specs=[
            pl.BlockSpec((tm, d), lambda i: (i, 0)),
            whole((1, d)),
            whole((d, n)),
            whole((1, n)),
        ],
        out_specs=pl.BlockSpec((tm, n), lambda i: (i, 0)),
        out_shape=jax.ShapeDtypeStruct((t, n), BF16),
        compiler_params=_params("parallel"),
        name="norm_proj",
    )(h, g.reshape(1, d), w, col_scale.reshape(1, n))


def _sb_kernel(q_ref, k_ref, v_ref, mask_ref, incl_ref, o_ref, pz_buf, sfx_buf, acc_ref,
               carry_ref, lst_q, lst_k, done_ref, *, tile, nq):
    half = LANES // 2
    lane = lax.broadcasted_iota(jnp.int32, (1, LANES), 1)
    head0 = lane < half

    @pl.when((pl.program_id(0) == 0) & (pl.program_id(1) == 0))
    def _():
        pz_buf[...] = jnp.zeros_like(pz_buf)
        sfx_buf[...] = jnp.zeros_like(sfx_buf)
        acc_ref[...] = jnp.zeros_like(acc_ref)
        carry_ref[...] = jnp.zeros_like(carry_ref)

    for i in range(nq + 1):
        done_ref[i] = 0

    def rows(idx):
        if isinstance(idx, int):
            return slice(idx * tile, (idx + 1) * tile)
        return pl.ds(pl.multiple_of(idx * tile, tile), tile)


    def stage_z(st, kj, slot, fresh, flag):
        suffix = sfx_buf[slot]
        if fresh is True:
            carry = suffix[:, 0:1]
            a = jnp.exp2(suffix + pz_buf[slot])
        else:
            carry = carry_ref[st] if fresh is False else jnp.where(fresh, 0.0, carry_ref[st])
            a = jnp.exp2(suffix + pz_buf[slot] + carry)
            carry = carry + suffix[:, 0:1]
        carry_ref[st] = carry
        if flag:
            done_ref[st] = (jnp.max(carry) < SKIP_LOG2).astype(jnp.int32)
        res = jnp.dot(a.astype(BF16), v_ref[rows(kj), :], preferred_element_type=F32)
        res = jnp.where(head0, res[:tile], res[tile:])
        if fresh is True:
            acc_ref[st] = res
        elif fresh is False:
            acc_ref[st] = res + acc_ref[st]
        else:
            acc_ref[st] = res + jnp.where(fresh, 0.0, acc_ref[st])

    def stage_y(slot):
        l = jnp.log(1.0 + jnp.exp2(pz_buf[slot])) * LOG2E
        sfx_buf[slot] = jnp.dot(l.astype(BF16), incl_ref[...], preferred_element_type=F32)

    def stage_x(qi, kj, slot, diag):
        q = q_ref[rows(qi), :]
        qzero = jnp.zeros_like(q)
        q2 = jnp.concatenate([jnp.where(head0, q, qzero), jnp.where(head0, qzero, q)], axis=0)
        pz_raw = lax.dot_general(q2, k_ref[rows(kj), :], (((1,), (1,)), ((), ())),
                                 preferred_element_type=F32)
        pz_buf[slot] = jnp.minimum(pz_raw, mask_ref[diag])

    first = [(qi, qi) for qi in range(nq)] + [(qi, qi - 1) for qi in range(1, nq)]
    for s in range(len(first) + 2):
        if s >= 2:
            qi, kj = first[s - 2]
            stage_z(qi, kj, s % 2, qi == kj, qi != kj)
        if 1 <= s <= len(first):
            stage_y((s - 1) % 2)
        if s < len(first):
            qi, kj = first[s]
            stage_x(qi, kj, s % 2, int(qi == kj))

    def entry(e):
        st = lst_q[e]
        return st, jnp.minimum(st, nq - 1), lst_k[e]

    def sub_step(e, slot):
        st, _, kj = entry(e)
        stage_z(st, kj, slot, st == nq, True)
        stage_y(1 - slot)
        _, qi, kj = entry(e + 2)
        stage_x(qi, kj, slot, 0)

    def put(i, st, kj):
        lst_q[i] = st
        lst_k[i] = kj

    def offset_body(state):
        d, _ = state
        put(0, nq, 0)
        put(1, nq, 0)
        n = 0
        for off in range(2):
            def add(qi, n, off=off):
                put(2 + n, qi, qi - d - off)
                return n + (done_ref[qi] == 0).astype(jnp.int32)

            n = lax.fori_loop(d + off, nq, add, n)
        for t in range(4):
            put(2 + n + t, nq, 0)

        def body(i, c):
            sub_step(2 * i, 0)
            sub_step(2 * i + 1, 1)
            return c

        lax.fori_loop(0, jnp.where(n > 0, (n + 3) // 2, 0), body, 0)
        return d + 2, n

    lax.while_loop(lambda st: (st[0] < nq) & (st[1] > 0), offset_body, (2, 1))

    for qi in range(nq):
        o_ref[qi * tile:(qi + 1) * tile, :] = acc_ref[qi].astype(o_ref.dtype)


def _sb_constants(tile):
    row = np.arange(tile)[:, None]
    col = np.arange(tile)[None, :]
    diag = np.where(col >= row, -MASK_BIG, PZ_MAX).astype(np.float32)
    diag = np.concatenate([diag, diag], axis=0)
    mask = np.stack([np.full_like(diag, PZ_MAX), diag])
    incl = np.where(row >= col, -1.0, 0.0).astype(np.float32)
    return jnp.asarray(mask), jnp.asarray(incl, dtype=BF16)


def _sb_attention(qkv, *, batch, seq, d_model):
    tile = SB_TILE
    nq = seq // tile
    pairs = SB_HEADS // 2
    qkv3 = qkv.reshape(batch, seq, 3 * d_model)
    mask, incl = _sb_constants(tile)
    kern = functools.partial(_sb_kernel, tile=tile, nq=nq)
    out = pl.pallas_call(
        kern,
        grid=(batch, pairs),
        in_specs=[
            pl.BlockSpec((None, seq, LANES), lambda b, p: (b, 0, p)),
            pl.BlockSpec((None, seq, LANES), lambda b, p: (b, 0, pairs + p)),
            pl.BlockSpec((None, seq, LANES), lambda b, p: (b, 0, 2 * pairs + p)),
            pl.BlockSpec((2, 2 * tile, tile), lambda b, p: (0, 0, 0), pipeline_mode=pl.Buffered(1)),
            pl.BlockSpec((tile, tile), lambda b, p: (0, 0), pipeline_mode=pl.Buffered(1)),
        ],
        out_specs=pl.BlockSpec((None, seq, LANES), lambda b, p: (b, 0, p)),
        out_shape=jax.ShapeDtypeStruct((batch, seq, d_model), BF16),
        scratch_shapes=[
            pltpu.VMEM((2, 2 * tile, tile), F32),
            pltpu.VMEM((2, 2 * tile, tile), F32),
            pltpu.VMEM((nq + 1, tile, LANES), F32),
            pltpu.VMEM((nq + 1, 2 * tile, 1), F32),
            pltpu.SMEM((2 * nq + 8,), jnp.int32),
            pltpu.SMEM((2 * nq + 8,), jnp.int32),
            pltpu.SMEM((nq + 1,), jnp.int32),
        ],
        compiler_params=_params("arbitrary", "arbitrary"),
        name="sb_attention",
    )(qkv3, qkv3, qkv3, mask, incl)
    return out.reshape(batch * seq, d_model)


def _ret_kernel(q_ref, k_ref, v_ref, g_ref, cos_ref, sin_ref, intra_ref, cross_ref,
                kv_ref, cd_ref, o_ref, state_ref, *, chunk, n_sub):
    @pl.when(pl.program_id(2) == 0)
    def _():
        state_ref[...] = jnp.zeros_like(state_ref)

    hd = cos_ref.shape[1]
    cross_d = cross_ref[...]
    kv_d = kv_ref[...]

    for c in range(n_sub):
        rs = slice(c * chunk, (c + 1) * chunk)
        cos = cos_ref[rs, :]
        sin = sin_ref[rs, :]

        def rope(x_ref):
            x = x_ref[rs, :].astype(F32)
            x1, x2 = x[:, :hd], x[:, hd:]
            return x1 * cos - x2 * sin, x1 * sin + x2 * cos

        q1, q2 = rope(q_ref)
        k1, k2 = rope(k_ref)
        v = v_ref[rs, :]

        qr = jnp.concatenate([q1, q2], axis=1).astype(BF16)
        kr = jnp.concatenate([k1, k2], axis=1).astype(BF16)
        scores = lax.dot_general(qr, kr, (((1,), (1,)), ((), ())),
                                 preferred_element_type=F32) * intra_ref[...]
        inner = jnp.dot(scores.astype(BF16), v, preferred_element_type=F32)

        state = state_ref[...]
        qc = jnp.concatenate([q1 * cross_d, q2 * cross_d], axis=1).astype(BF16)
        cross = jnp.dot(qc, state.astype(BF16), preferred_element_type=F32)

        kd = jnp.concatenate([k1 * kv_d, k2 * kv_d], axis=1).astype(BF16)
        state_ref[...] = state * cd_ref[...] + lax.dot_general(
            kd, v, (((0,), (0,)), ((), ())), preferred_element_type=F32)

        y = inner + cross
        y = y * lax.rsqrt(jnp.mean(y * y, axis=-1, keepdims=True) + EPS)
        hg = 0.5 * g_ref[rs, :].astype(F32)
        o_ref[rs, :] = ((hg + hg * jnp.tanh(hg)) * y).astype(o_ref.dtype)


def _retention_tables(seq, key_dim, val_dim, chunk):
    half = key_dim // 2
    k_scale = key_dim ** -0.5
    inv_freq = ROPE_BASE ** (-np.arange(0, key_dim, 2, dtype=np.float64) / key_dim)
    ang = np.arange(seq, dtype=np.float64)[:, None] * inv_freq[None, :]
    log_gamma = np.log1p(-np.exp2(-5.0 - np.arange(RET_HEADS, dtype=np.float64)))
    idx = np.arange(chunk, dtype=np.float64)
    diff = idx[:, None] - idx[None, :]
    intra = np.where(diff >= 0, np.exp(log_gamma[:, None, None] * np.maximum(diff, 0.0)), 0.0)
    cross = np.exp(log_gamma[:, None] * (idx + 1.0))
    kv = np.exp(log_gamma[:, None] * (chunk - 1.0 - idx))
    cd = np.exp(log_gamma * chunk)
    tables = dict(
        cos=np.cos(ang), sin=np.sin(ang), intra=intra * k_scale,
        cross=np.broadcast_to(cross[:, :, None], (RET_HEADS, chunk, half)),
        kv=np.broadcast_to((kv * k_scale)[:, :, None], (RET_HEADS, chunk, half)),
        cd=np.broadcast_to(cd[:, None, None], (RET_HEADS, 1, val_dim)),
    )
    return {name: jnp.asarray(t, dtype=F32) for name, t in tables.items()}


def _retention(proj, *, batch, seq, d_model):
    chunk = RET_CHUNK
    key_dim = d_model // RET_HEADS
    val_dim = 2 * d_model // RET_HEADS
    half = key_dim // 2
    tb = _retention_tables(seq, key_dim, val_dim, chunk)
    proj3 = proj.reshape(batch, seq, 6 * d_model)
    nh = RET_HEADS
    n_sub = RET_CHUNKS_PER_STEP
    blk = chunk * n_sub
    kern = functools.partial(_ret_kernel, chunk=chunk, n_sub=n_sub)
    out = pl.pallas_call(
        kern,
        grid=(batch, nh, seq // blk),
        in_specs=[
            pl.BlockSpec((None, blk, key_dim), lambda b, h, c: (b, c, h)),
            pl.BlockSpec((None, blk, key_dim), lambda b, h, c: (b, c, nh + h)),
            pl.BlockSpec((None, blk, val_dim), lambda b, h, c: (b, c, nh + h)),
            pl.BlockSpec((None, blk, val_dim), lambda b, h, c: (b, c, 2 * nh + h)),
            pl.BlockSpec((blk, half), lambda b, h, c: (c, 0)),
            pl.BlockSpec((blk, half), lambda b, h, c: (c, 0)),
            pl.BlockSpec((None, chunk, chunk), lambda b, h, c: (h, 0, 0)),
            pl.BlockSpec((None, chunk, half), lambda b, h, c: (h, 0, 0)),
            pl.BlockSpec((None, chunk, half), lambda b, h, c: (h, 0, 0)),
            pl.BlockSpec((None, 1, val_dim), lambda b, h, c: (h, 0, 0)),
        ],
        out_specs=pl.BlockSpec((None, blk, val_dim), lambda b, h, c: (b, c, h)),
        out_shape=jax.ShapeDtypeStruct((batch, seq, nh * val_dim), BF16),
        scratch_shapes=[pltpu.VMEM((key_dim, val_dim), F32)],
        compiler_params=_params("parallel", "parallel", "arbitrary"),
        name="retention",
    )(proj3, proj3, proj3, proj3, tb["cos"], tb["sin"], tb["intra"], tb["cross"], tb["kv"], tb["cd"])
    return out.reshape(batch * seq, nh * val_dim)


def _tail_kernel(h_ref, mix_ref, p_ref, wo_ref, gm_ref, wu_ref, wd_ref, gp_ref, wg_ref, wp_ref,
                 fg_ref, o_ref, *, final_norm, tf):
    h1 = h_ref[...] + jnp.dot(mix_ref[...], wo_ref[...], preferred_element_type=F32)
    xn = _rms_norm(h1, gm_ref[...]).astype(BF16)
    h2 = h1
    for c in range(wu_ref.shape[1] // tf):
        a = jnp.maximum(jnp.dot(xn, wu_ref[:, c * tf:(c + 1) * tf], preferred_element_type=F32), 0.0)
        h2 = h2 + jnp.dot((a * a).astype(BF16), wd_ref[c * tf:(c + 1) * tf, :],
                          preferred_element_type=F32)
    xn = _rms_norm(h2, gp_ref[...]).astype(BF16)
    gate = jax.nn.sigmoid(jnp.dot(xn, wg_ref[...], preferred_element_type=F32))
    up = jnp.dot(p_ref[...].astype(BF16), wp_ref[...], preferred_element_type=F32)
    out = h2 + up * gate
    if final_norm:
        out = _rms_norm(out, fg_ref[...])
    o_ref[...] = out


def _tail(h, mix, p, layer, w_out, g_mlp, w_up, w_down, g_ple, w_gate, w_ple, final_g, *, final_norm):
    t, d = h.shape
    dm = mix.shape[1]
    dp = p.shape[2]
    f = w_up.shape[1]
    tm = TAIL_ROW_TILE

    def whole(shape):
        return pl.BlockSpec(shape, lambda i: (0, 0), pipeline_mode=pl.Buffered(1))

    return pl.pallas_call(
        functools.partial(_tail_kernel, final_norm=final_norm, tf=MLP_FF_TILE),
        grid=(t // tm,),
        in_specs=[
            pl.BlockSpec((tm, d), lambda i: (i, 0)),
            pl.BlockSpec((tm, dm), lambda i: (i, 0)),
            pl.BlockSpec((None, tm, dp), lambda i: (layer, i, 0)),
            whole((dm, d)),
            whole((1, d)),
            whole((d, f)),
            whole((f, d)),
            whole((1, d)),
            whole((d, d)),
            whole((dp, d)),
            whole((1, d)),
        ],
        out_specs=pl.BlockSpec((tm, d), lambda i: (i, 0)),
        out_shape=jax.ShapeDtypeStruct((t, d), F32),
        compiler_params=_params("parallel"),
        name="tail",
    )(h, mix, p, w_out, g_mlp.reshape(1, d), w_up, w_down, g_ple.reshape(1, d), w_gate, w_ple,
      final_g.reshape(1, d))


def kernel(x, p, mix_norm, sb_w_in, sb_w_out, ret_w_in, ret_w_out, mlp_norm, mlp_w_up,
           mlp_w_down, ple_norm, ple_w_gate, ple_w_up, final_norm):
    batch, seq, d_model = x.shape
    depth = p.shape[0]
    t = batch * seq
    h = x.reshape(t, d_model)
    p2 = p.reshape(depth, t, p.shape[-1])
    for i in range(depth):
        if i % 2 == 0:
            q_scale = LOG2E * (d_model // SB_HEADS) ** -0.5
            col_scale = jnp.concatenate([jnp.full((d_model,), q_scale, F32), jnp.ones((2 * d_model,), F32)])
            qkv = _norm_proj(h, mix_norm[i], sb_w_in[i // 2].astype(BF16), col_scale)
            mix = _sb_attention(qkv, batch=batch, seq=seq, d_model=d_model)
            w_out = sb_w_out[i // 2]
        else:
            proj = _norm_proj(h, mix_norm[i], ret_w_in[i // 2].astype(BF16), jnp.ones((6 * d_model,), F32))
            mix = _retention(proj, batch=batch, seq=seq, d_model=d_model)
            w_out = ret_w_out[i // 2]
        h = _tail(h, mix, p2, i, w_out.astype(BF16), mlp_norm[i], mlp_w_up[i].astype(BF16),
                  mlp_w_down[i].astype(BF16), ple_norm[i], ple_w_gate[i].astype(BF16),
                  ple_w_up[i].astype(BF16), final_norm, final_norm=(i == depth - 1))
    return h.reshape(batch, seq, d_model)
```

```python
import functools
import math

import jax
import jax.numpy as jnp
import numpy as np
from jax import lax
from jax.experimental import pallas as pl
from jax.experimental.pallas import tpu as pltpu

F32 = jnp.float32
BF16 = jnp.bfloat16

EPS = 1e-6
SB_HEADS = 16
RET_HEADS = 4
ROPE_BASE = 10000.0

VMEM_LIMIT_BYTES = 56 * 1024 * 1024
LANES = 128
LOG2E = math.log2(math.e)
MASK_BIG = 1e30
PZ_MAX = 126.0
SKIP_LOG2 = -160.0

PROJ_ROW_TILE = 1024
SB_PROJ_COL_TILE = 1536
RET_PROJ_COL_TILE = 1024
TAIL_ROW_TILE = 512
MLP_FF_TILE = 1024
SB_TILE = 256
RET_CHUNK = 256
RET_CHUNKS_PER_STEP = 16


def _params(*sem):
    return pltpu.CompilerParams(dimension_semantics=sem, vmem_limit_bytes=VMEM_LIMIT_BYTES)


def _rms_norm(x, g):
    ms = jnp.mean(x * x, axis=-1, keepdims=True)
    return x * lax.rsqrt(ms + EPS) * g


def _norm_proj_kernel(*refs, tn, scaled, rope_chunks):
    x_ref, g_ref, w_ref = refs[:3]
    o_ref = refs[-1]
    xn = _rms_norm(x_ref[...], g_ref[...]).astype(BF16)
    if rope_chunks:
        cos = refs[3][...]
        sin = refs[4][...]
        hd = cos.shape[1]
    for c in range(w_ref.shape[1] // tn):
        cols = slice(c * tn, (c + 1) * tn)
        acc = jnp.dot(xn, w_ref[:, cols], preferred_element_type=F32)
        if scaled:
            acc = acc * refs[3][:, cols]
        if c < rope_chunks:
            for h in range(tn // (2 * hd)):
                lo = c * tn + 2 * h * hd
                x1 = acc[:, 2 * h * hd:(2 * h + 1) * hd]
                x2 = acc[:, (2 * h + 1) * hd:(2 * h + 2) * hd]
                o_ref[:, lo:lo + hd] = (x1 * cos - x2 * sin).astype(o_ref.dtype)
                o_ref[:, lo + hd:lo + 2 * hd] = (x1 * sin + x2 * cos).astype(o_ref.dtype)
        else:
            o_ref[:, cols] = acc.astype(o_ref.dtype)


def _norm_proj(h, g, w, *, tn, col_scale=None, rope=None, seq=None):
    t, d = h.shape
    n = w.shape[1]
    tm = PROJ_ROW_TILE

    def whole(shape):
        return pl.BlockSpec(shape, lambda i: (0, 0), pipeline_mode=pl.Buffered(1))

    in_specs = [pl.BlockSpec((tm, d), lambda i: (i, 0)), whole((1, d)), whole((d, n))]
    args = [h, g.reshape(1, d), w]
    rope_chunks = 0
    if col_scale is not None:
        in_specs.append(whole((1, n)))
        args.append(col_scale.reshape(1, n))
    if rope is not None:
        cos, sin, rope_cols = rope
        rope_chunks = rope_cols // tn
        tiles_per_seq = seq // tm
        for tbl in (cos, sin):
            in_specs.append(pl.BlockSpec((tm, tbl.shape[1]), lambda i: (i % tiles_per_seq, 0)))
            args.append(tbl)
    return pl.pallas_call(
        functools.partial(_norm_proj_kernel, tn=tn, scaled=col_scale is not None,
                          rope_chunks=rope_chunks),
        grid=(t // tm,),
        in_specs=in_specs,
        out_specs=pl.BlockSpec((tm, n), lambda i: (i, 0)),
        out_shape=jax.ShapeDtypeStruct((t, n), BF16),
        compiler_params=_params("parallel"),
        name="norm_proj",
    )(*args)


def _sb_kernel(q_ref, k_ref, v_ref, mask_ref, incl_ref, o_ref, pz_buf, sfx_buf, acc_ref,
               carry_ref, lst_q, lst_k, done_ref, *, tile, nq):
    half = LANES // 2
    lane = lax.broadcasted_iota(jnp.int32, (1, LANES), 1)
    head0 = lane < half

    @pl.when((pl.program_id(0) == 0) & (pl.program_id(1) == 0))
    def _():
        pz_buf[...] = jnp.zeros_like(pz_buf)
        sfx_buf[...] = jnp.zeros_like(sfx_buf)
        acc_ref[...] = jnp.zeros_like(acc_ref)
        carry_ref[...] = jnp.zeros_like(carry_ref)

    for i in range(nq + 1):
        done_ref[i] = 0

    def rows(idx):
        if isinstance(idx, int):
            return slice(idx * tile, (idx + 1) * tile)
        return pl.ds(pl.multiple_of(idx * tile, tile), tile)


    def stage_z(st, kj, slot, fresh, flag):
        suffix = sfx_buf[slot]
        if fresh is True:
            carry = suffix[:, 0:1]
            a = jnp.exp2(suffix + pz_buf[slot])
        else:
            carry = carry_ref[st] if fresh is False else jnp.where(fresh, 0.0, carry_ref[st])
            a = jnp.exp2(suffix + pz_buf[slot] + carry)
            carry = carry + suffix[:, 0:1]
        carry_ref[st] = carry
        if flag:
            done_ref[st] = (jnp.max(carry) < SKIP_LOG2).astype(jnp.int32)
        res = jnp.dot(a.astype(BF16), v_ref[rows(kj), :], preferred_element_type=F32)
        res = jnp.where(head0, res[:tile], res[tile:])
        if fresh is True:
            acc_ref[st] = res
        elif fresh is False:
            acc_ref[st] = res + acc_ref[st]
        else:
            acc_ref[st] = res + jnp.where(fresh, 0.0, acc_ref[st])

    def stage_y(slot):
        l = jnp.log(1.0 + jnp.exp2(pz_buf[slot])) * LOG2E
        sfx_buf[slot] = jnp.dot(l.astype(BF16), incl_ref[...], preferred_element_type=F32)

    def stage_x(qi, kj, slot, diag):
        q = q_ref[rows(qi), :]
        qzero = jnp.zeros_like(q)
        q2 = jnp.concatenate([jnp.where(head0, q, qzero), jnp.where(head0, qzero, q)], axis=0)
        pz_raw = lax.dot_general(q2, k_ref[rows(kj), :], (((1,), (1,)), ((), ())),
                                 preferred_element_type=F32)
        pz_buf[slot] = jnp.minimum(pz_raw, mask_ref[diag])

    first = [(qi, qi) for qi in range(nq)] + [(qi, qi - 1) for qi in range(1, nq)]
    for s in range(len(first) + 2):
        if s >= 2:
            qi, kj = first[s - 2]
            stage_z(qi, kj, s % 2, qi == kj, qi != kj)
        if 1 <= s <= len(first):
            stage_y((s - 1) % 2)
        if s < len(first):
            qi, kj = first[s]
            stage_x(qi, kj, s % 2, int(qi == kj))

    def entry(e):
        st = lst_q[e]
        return st, jnp.minimum(st, nq - 1), lst_k[e]

    def sub_step(e, slot):
        st, _, kj = entry(e)
        stage_z(st, kj, slot, st == nq, True)
        stage_y(1 - slot)
        _, qi, kj = entry(e + 2)
        stage_x(qi, kj, slot, 0)

    def put(i, st, kj):
        lst_q[i] = st
        lst_k[i] = kj

    def offset_body(state):
        d, _ = state
        put(0, nq, 0)
        put(1, nq, 0)
        n = 0
        for off in range(2):
            def add(qi, n, off=off):
                put(2 + n, qi, qi - d - off)
                return n + (done_ref[qi] == 0).astype(jnp.int32)

            n = lax.fori_loop(d + off, nq, add, n)
        for t in range(4):
            put(2 + n + t, nq, 0)

        def body(i, c):
            sub_step(2 * i, 0)
            sub_step(2 * i + 1, 1)
            return c

        lax.fori_loop(0, jnp.where(n > 0, (n + 3) // 2, 0), body, 0)
        return d + 2, n

    lax.while_loop(lambda st: (st[0] < nq) & (st[1] > 0), offset_body, (2, 1))

    for qi in range(nq):
        o_ref[qi * tile:(qi + 1) * tile, :] = acc_ref[qi].astype(o_ref.dtype)


def _sb_constants(tile):
    row = np.arange(tile)[:, None]
    col = np.arange(tile)[None, :]
    diag = np.where(col >= row, -MASK_BIG, PZ_MAX).astype(np.float32)
    diag = np.concatenate([diag, diag], axis=0)
    mask = np.stack([np.full_like(diag, PZ_MAX), diag])
    incl = np.where(row >= col, -1.0, 0.0).astype(np.float32)
    return jnp.asarray(mask), jnp.asarray(incl, dtype=BF16)


def _sb_attention(qkv, *, batch, seq, d_model):
    tile = SB_TILE
    nq = seq // tile
    pairs = SB_HEADS // 2
    qkv3 = qkv.reshape(batch, seq, 3 * d_model)
    mask, incl = _sb_constants(tile)
    kern = functools.partial(_sb_kernel, tile=tile, nq=nq)
    out = pl.pallas_call(
        kern,
        grid=(batch, pairs),
        in_specs=[
            pl.BlockSpec((None, seq, LANES), lambda b, p: (b, 0, p)),
            pl.BlockSpec((None, seq, LANES), lambda b, p: (b, 0, pairs + p)),
            pl.BlockSpec((None, seq, LANES), lambda b, p: (b, 0, 2 * pairs + p)),
            pl.BlockSpec((2, 2 * tile, tile), lambda b, p: (0, 0, 0), pipeline_mode=pl.Buffered(1)),
            pl.BlockSpec((tile, tile), lambda b, p: (0, 0), pipeline_mode=pl.Buffered(1)),
        ],
        out_specs=pl.BlockSpec((None, seq, LANES), lambda b, p: (b, 0, p)),
        out_shape=jax.ShapeDtypeStruct((batch, seq, d_model), BF16),
        scratch_shapes=[
            pltpu.VMEM((2, 2 * tile, tile), F32),
            pltpu.VMEM((2, 2 * tile, tile), F32),
            pltpu.VMEM((nq + 1, tile, LANES), F32),
            pltpu.VMEM((nq + 1, 2 * tile, 1), F32),
            pltpu.SMEM((2 * nq + 8,), jnp.int32),
            pltpu.SMEM((2 * nq + 8,), jnp.int32),
            pltpu.SMEM((nq + 1,), jnp.int32),
        ],
        compiler_params=_params("arbitrary", "arbitrary"),
        name="sb_attention",
    )(qkv3, qkv3, qkv3, mask, incl)
    return out.reshape(batch * seq, d_model)


def _ret_kernel(q_ref, k_ref, v_ref, g_ref, intra_ref, cross_ref, kv_ref, cd_ref, o_ref,
                state_ref, *, chunk, n_sub):
    @pl.when(pl.program_id(2) == 0)
    def _():
        state_ref[...] = jnp.zeros_like(state_ref)

    cross_d = cross_ref[...]
    kv_d = kv_ref[...]

    for c in range(n_sub):
        rs = slice(c * chunk, (c + 1) * chunk)
        qr = q_ref[rs, :]
        kr = k_ref[rs, :]
        v = v_ref[rs, :]

        scores = lax.dot_general(qr, kr, (((1,), (1,)), ((), ())),
                                 preferred_element_type=F32) * intra_ref[...]
        inner = jnp.dot(scores.astype(BF16), v, preferred_element_type=F32)

        state = state_ref[...]
        qc = (qr.astype(F32) * cross_d).astype(BF16)
        cross = jnp.dot(qc, state.astype(BF16), preferred_element_type=F32)

        kd = (kr.astype(F32) * kv_d).astype(BF16)
        state_ref[...] = state * cd_ref[...] + lax.dot_general(
            kd, v, (((0,), (0,)), ((), ())), preferred_element_type=F32)

        y = inner + cross
        y = y * lax.rsqrt(jnp.mean(y * y, axis=-1, keepdims=True) + EPS)
        hg = 0.5 * g_ref[rs, :].astype(F32)
        o_ref[rs, :] = ((hg + hg * jnp.tanh(hg)) * y).astype(o_ref.dtype)


def _retention_tables(seq, key_dim, val_dim, chunk):
    k_scale = key_dim ** -0.5
    inv_freq = ROPE_BASE ** (-np.arange(0, key_dim, 2, dtype=np.float64) / key_dim)
    ang = np.arange(seq, dtype=np.float64)[:, None] * inv_freq[None, :]
    log_gamma = np.log1p(-np.exp2(-5.0 - np.arange(RET_HEADS, dtype=np.float64)))
    idx = np.arange(chunk, dtype=np.float64)
    diff = idx[:, None] - idx[None, :]
    intra = np.where(diff >= 0, np.exp(log_gamma[:, None, None] * np.maximum(diff, 0.0)), 0.0)
    cross = np.exp(log_gamma[:, None] * (idx + 1.0))
    kv = np.exp(log_gamma[:, None] * (chunk - 1.0 - idx))
    cd = np.exp(log_gamma * chunk)
    tables = dict(
        cos=np.cos(ang), sin=np.sin(ang), intra=intra * k_scale,
        cross=np.broadcast_to(cross[:, :, None], (RET_HEADS, chunk, key_dim)),
        kv=np.broadcast_to((kv * k_scale)[:, :, None], (RET_HEADS, chunk, key_dim)),
        cd=np.broadcast_to(cd[:, None, None], (RET_HEADS, 1, val_dim)),
    )
    return {name: jnp.asarray(t, dtype=F32) for name, t in tables.items()}


def _retention(proj, tb, *, batch, seq, d_model):
    chunk = RET_CHUNK
    key_dim = d_model // RET_HEADS
    val_dim = 2 * d_model // RET_HEADS
    proj3 = proj.reshape(batch, seq, 6 * d_model)
    nh = RET_HEADS
    n_sub = RET_CHUNKS_PER_STEP
    blk = chunk * n_sub
    kern = functools.partial(_ret_kernel, chunk=chunk, n_sub=n_sub)
    out = pl.pallas_call(
        kern,
        grid=(batch, nh, seq // blk),
        in_specs=[
            pl.BlockSpec((None, blk, key_dim), lambda b, h, c: (b, c, h)),
            pl.BlockSpec((None, blk, key_dim), lambda b, h, c: (b, c, nh + h)),
            pl.BlockSpec((None, blk, val_dim), lambda b, h, c: (b, c, nh + h)),
            pl.BlockSpec((None, blk, val_dim), lambda b, h, c: (b, c, 2 * nh + h)),
            pl.BlockSpec((None, chunk, chunk), lambda b, h, c: (h, 0, 0)),
            pl.BlockSpec((None, chunk, key_dim), lambda b, h, c: (h, 0, 0)),
            pl.BlockSpec((None, chunk, key_dim), lambda b, h, c: (h, 0, 0)),
            pl.BlockSpec((None, 1, val_dim), lambda b, h, c: (h, 0, 0)),
        ],
        out_specs=pl.BlockSpec((None, blk, val_dim), lambda b, h, c: (b, c, h)),
        out_shape=jax.ShapeDtypeStruct((batch, seq, nh * val_dim), BF16),
        scratch_shapes=[pltpu.VMEM((key_dim, val_dim), F32)],
        compiler_params=_params("parallel", "parallel", "arbitrary"),
        name="retention",
    )(proj3, proj3, proj3, proj3, tb["intra"], tb["cross"], tb["kv"], tb["cd"])
    return out.reshape(batch * seq, nh * val_dim)


def _tail_kernel(h_ref, mix_ref, p_ref, wo_ref, gm_ref, wu_ref, wd_ref, gp_ref, wg_ref, wp_ref,
                 fg_ref, o_ref, *, final_norm, tf):
    h1 = h_ref[...] + jnp.dot(mix_ref[...], wo_ref[...], preferred_element_type=F32)
    xn = _rms_norm(h1, gm_ref[...]).astype(BF16)
    h2 = h1
    for c in range(wu_ref.shape[1] // tf):
        a = jnp.maximum(jnp.dot(xn, wu_ref[:, c * tf:(c + 1) * tf], preferred_element_type=F32), 0.0)
        h2 = h2 + jnp.dot((a * a).astype(BF16), wd_ref[c * tf:(c + 1) * tf, :],
                          preferred_element_type=F32)
    xn = _rms_norm(h2, gp_ref[...]).astype(BF16)
    gate = jax.nn.sigmoid(jnp.dot(xn, wg_ref[...], preferred_element_type=F32))
    up = jnp.dot(p_ref[...].astype(BF16), wp_ref[...], preferred_element_type=F32)
    out = h2 + up * gate
    if final_norm:
        out = _rms_norm(out, fg_ref[...])
    o_ref[...] = out


def _tail(h, mix, p, layer, w_out, g_mlp, w_up, w_down, g_ple, w_gate, w_ple, final_g, *, final_norm):
    t, d = h.shape
    dm = mix.shape[1]
    dp = p.shape[2]
    f = w_up.shape[1]
    tm = TAIL_ROW_TILE

    def whole(shape):
        return pl.BlockSpec(shape, lambda i: (0, 0), pipeline_mode=pl.Buffered(1))

    return pl.pallas_call(
        functools.partial(_tail_kernel, final_norm=final_norm, tf=MLP_FF_TILE),
        grid=(t // tm,),
        in_specs=[
            pl.BlockSpec((tm, d), lambda i: (i, 0)),
            pl.BlockSpec((tm, dm), lambda i: (i, 0)),
            pl.BlockSpec((None, tm, dp), lambda i: (layer, i, 0)),
            whole((dm, d)),
            whole((1, d)),
            whole((d, f)),
            whole((f, d)),
            whole((1, d)),
            whole((d, d)),
            whole((dp, d)),
            whole((1, d)),
        ],
        out_specs=pl.BlockSpec((tm, d), lambda i: (i, 0)),
        out_shape=jax.ShapeDtypeStruct((t, d), F32),
        compiler_params=_params("parallel"),
        name="tail",
    )(h, mix, p, w_out, g_mlp.reshape(1, d), w_up, w_down, g_ple.reshape(1, d), w_gate, w_ple,
      final_g.reshape(1, d))


def kernel(x, p, mix_norm, sb_w_in, sb_w_out, ret_w_in, ret_w_out, mlp_norm, mlp_w_up,
           mlp_w_down, ple_norm, ple_w_gate, ple_w_up, final_norm):
    batch, seq, d_model = x.shape
    depth = p.shape[0]
    t = batch * seq
    h = x.reshape(t, d_model)
    p2 = p.reshape(depth, t, p.shape[-1])
    for i in range(depth):
        if i % 2 == 0:
            q_scale = LOG2E * (d_model // SB_HEADS) ** -0.5
            col_scale = jnp.concatenate([jnp.full((d_model,), q_scale, F32), jnp.ones((2 * d_model,), F32)])
            qkv = _norm_proj(h, mix_norm[i], sb_w_in[i // 2].astype(BF16), tn=SB_PROJ_COL_TILE,
                             col_scale=col_scale)
            mix = _sb_attention(qkv, batch=batch, seq=seq, d_model=d_model)
            w_out = sb_w_out[i // 2]
        else:
            tb = _retention_tables(seq, d_model // RET_HEADS, 2 * d_model // RET_HEADS, RET_CHUNK)
            proj = _norm_proj(h, mix_norm[i], ret_w_in[i // 2].astype(BF16), tn=RET_PROJ_COL_TILE,
                              rope=(tb["cos"], tb["sin"], 2 * d_model), seq=seq)
            mix = _retention(proj, tb, batch=batch, seq=seq, d_model=d_model)
            w_out = ret_w_out[i // 2]
        h = _tail(h, mix, p2, i, w_out.astype(BF16), mlp_norm[i], mlp_w_up[i].astype(BF16),
                  mlp_w_down[i].astype(BF16), ple_norm[i], ple_w_gate[i].astype(BF16),
                  ple_w_up[i].astype(BF16), final_norm, final_norm=(i == depth - 1))
    return h.reshape(batch, seq, d_model)
```

```python
import functools
import math

import jax
import jax.numpy as jnp
import numpy as np
from jax import lax
from jax.experimental import pallas as pl
from jax.experimental.pallas import tpu as pltpu

F32 = jnp.float32
BF16 = jnp.bfloat16

EPS = 1e-6
SB_HEADS = 16
RET_HEADS = 4
ROPE_BASE = 10000.0

VMEM_LIMIT_BYTES = 56 * 1024 * 1024
LANES = 128
LOG2E = math.log2(math.e)
MASK_BIG = 1e30
PZ_MAX = 126.0
SKIP_LOG2 = -160.0

PROJ_ROW_TILE = 1024
SB_PROJ_COL_TILE = 1536
RET_PROJ_COL_TILE = 1024
TAIL_ROW_TILE = 512
MLP_FF_TILE = 1024
SB_TILE = 256
SB_LAG = 2
SB_SLOTS = 6
RET_CHUNK = 256
RET_CHUNKS_PER_STEP = 16


def _params(*sem):
    return pltpu.CompilerParams(dimension_semantics=sem, vmem_limit_bytes=VMEM_LIMIT_BYTES)


def _rms_norm(x, g):
    ms = jnp.mean(x * x, axis=-1, keepdims=True)
    return x * lax.rsqrt(ms + EPS) * g


def _norm_proj_kernel(*refs, tn, scaled, rope_chunks, silu_from):
    x_ref, g_ref, w_ref = refs[:3]
    o_ref = refs[-1]
    xn = _rms_norm(x_ref[...], g_ref[...]).astype(BF16)
    if rope_chunks:
        cos = refs[3][...]
        sin = refs[4][...]
        hd = cos.shape[1]
    for c in range(w_ref.shape[1] // tn):
        cols = slice(c * tn, (c + 1) * tn)
        acc = jnp.dot(xn, w_ref[:, cols], preferred_element_type=F32)
        if scaled:
            acc = acc * refs[3][:, cols]
        if silu_from is not None and c >= silu_from:
            hx = 0.5 * acc
            o_ref[:, cols] = (hx + hx * jnp.tanh(hx)).astype(o_ref.dtype)
        elif c < rope_chunks:
            for h in range(tn // (2 * hd)):
                lo = c * tn + 2 * h * hd
                x1 = acc[:, 2 * h * hd:(2 * h + 1) * hd]
                x2 = acc[:, (2 * h + 1) * hd:(2 * h + 2) * hd]
                o_ref[:, lo:lo + hd] = (x1 * cos - x2 * sin).astype(o_ref.dtype)
                o_ref[:, lo + hd:lo + 2 * hd] = (x1 * sin + x2 * cos).astype(o_ref.dtype)
        else:
            o_ref[:, cols] = acc.astype(o_ref.dtype)


def _norm_proj(h, g, w, *, tn, col_scale=None, rope=None, seq=None, silu_cols=None):
    t, d = h.shape
    n = w.shape[1]
    tm = PROJ_ROW_TILE

    def whole(shape):
        return pl.BlockSpec(shape, lambda i: (0, 0), pipeline_mode=pl.Buffered(1))

    in_specs = [pl.BlockSpec((tm, d), lambda i: (i, 0)), whole((1, d)), whole((d, n))]
    args = [h, g.reshape(1, d), w]
    rope_chunks = 0
    if col_scale is not None:
        in_specs.append(whole((1, n)))
        args.append(col_scale.reshape(1, n))
    if rope is not None:
        cos, sin, rope_cols = rope
        rope_chunks = rope_cols // tn
        tiles_per_seq = seq // tm
        for tbl in (cos, sin):
            in_specs.append(pl.BlockSpec((tm, tbl.shape[1]), lambda i: (i % tiles_per_seq, 0)))
            args.append(tbl)
    return pl.pallas_call(
        functools.partial(_norm_proj_kernel, tn=tn, scaled=col_scale is not None,
                          rope_chunks=rope_chunks,
                          silu_from=None if silu_cols is None else (n - silu_cols) // tn),
        grid=(t // tm,),
        in_specs=in_specs,
        out_specs=pl.BlockSpec((tm, n), lambda i: (i, 0)),
        out_shape=jax.ShapeDtypeStruct((t, n), BF16),
        compiler_params=_params("parallel"),
        name="norm_proj",
    )(*args)


def _sb_kernel(q_ref, k_ref, v_ref, mask_ref, incl_ref, o_ref, pz_buf, sfx_buf, acc_ref,
               carry_ref, lst_q, lst_k, done_ref, *, tile, nq):
    half = LANES // 2
    lane = lax.broadcasted_iota(jnp.int32, (1, LANES), 1)
    head0 = lane < half

    @pl.when((pl.program_id(0) == 0) & (pl.program_id(1) == 0))
    def _():
        pz_buf[...] = jnp.zeros_like(pz_buf)
        sfx_buf[...] = jnp.zeros_like(sfx_buf)
        acc_ref[...] = jnp.zeros_like(acc_ref)
        carry_ref[...] = jnp.zeros_like(carry_ref)

    for i in range(nq + 1):
        done_ref[i] = 0

    def rows(idx):
        if isinstance(idx, int):
            return slice(idx * tile, (idx + 1) * tile)
        return pl.ds(pl.multiple_of(idx * tile, tile), tile)


    def stage_z(st, kj, slot, fresh, flag):
        suffix = sfx_buf[slot]
        if fresh is True:
            carry = suffix[:, 0:1]
            a = jnp.exp2(suffix + pz_buf[slot])
        else:
            carry = carry_ref[st] if fresh is False else jnp.where(fresh, 0.0, carry_ref[st])
            a = jnp.exp2(suffix + pz_buf[slot] + carry)
            carry = carry + suffix[:, 0:1]
        carry_ref[st] = carry
        if flag:
            done_ref[st] = (jnp.max(carry) < SKIP_LOG2).astype(jnp.int32)
        res = jnp.dot(a.astype(BF16), v_ref[rows(kj), :], preferred_element_type=F32)
        res = jnp.where(head0, res[:tile], res[tile:])
        if fresh is True:
            acc_ref[st] = res
        elif fresh is False:
            acc_ref[st] = res + acc_ref[st]
        else:
            acc_ref[st] = res + jnp.where(fresh, 0.0, acc_ref[st])

    def stage_y(slot):
        l = jnp.log(1.0 + jnp.exp2(pz_buf[slot])) * LOG2E
        sfx_buf[slot] = jnp.dot(l.astype(BF16), incl_ref[...], preferred_element_type=F32)

    def stage_x(qi, kj, slot, diag):
        q = q_ref[rows(qi), :]
        qzero = jnp.zeros_like(q)
        q2 = jnp.concatenate([jnp.where(head0, q, qzero), jnp.where(head0, qzero, q)], axis=0)
        pz_raw = lax.dot_general(q2, k_ref[rows(kj), :], (((1,), (1,)), ((), ())),
                                 preferred_element_type=F32)
        pz_buf[slot] = jnp.minimum(pz_raw, mask_ref[diag])

    first = [(qi, qi) for qi in range(nq)] + [(qi, qi - 1) for qi in range(1, nq)]
    for s in range(len(first) + 2 * SB_LAG):
        if s >= 2 * SB_LAG:
            qi, kj = first[s - 2 * SB_LAG]
            stage_z(qi, kj, (s - 2 * SB_LAG) % SB_SLOTS, qi == kj, qi != kj)
        if SB_LAG <= s < len(first) + SB_LAG:
            stage_y((s - SB_LAG) % SB_SLOTS)
        if s < len(first):
            qi, kj = first[s]
            stage_x(qi, kj, s % SB_SLOTS, int(qi == kj))

    def entry(e):
        st = lst_q[e]
        return st, jnp.minimum(st, nq - 1), lst_k[e]

    def sub_step(e, slot):
        st, _, kj = entry(e)
        stage_z(st, kj, slot, st == nq, True)
        stage_y(1 - slot)
        _, qi, kj = entry(e + 2)
        stage_x(qi, kj, slot, 0)

    def put(i, st, kj):
        lst_q[i] = st
        lst_k[i] = kj

    def offset_body(state):
        d, _ = state
        put(0, nq, 0)
        put(1, nq, 0)
        n = 0
        for off in range(2):
            def add(qi, n, off=off):
                put(2 + n, qi, qi - d - off)
                return n + (done_ref[qi] == 0).astype(jnp.int32)

            n = lax.fori_loop(d + off, nq, add, n)
        for t in range(4):
            put(2 + n + t, nq, 0)

        def body(i, c):
            sub_step(2 * i, 0)
            sub_step(2 * i + 1, 1)
            return c

        lax.fori_loop(0, jnp.where(n > 0, (n + 3) // 2, 0), body, 0)
        return d + 2, n

    lax.while_loop(lambda st: (st[0] < nq) & (st[1] > 0), offset_body, (2, 1))

    for qi in range(nq):
        o_ref[qi * tile:(qi + 1) * tile, :] = acc_ref[qi].astype(o_ref.dtype)


def _sb_constants(tile):
    row = np.arange(tile)[:, None]
    col = np.arange(tile)[None, :]
    diag = np.where(col >= row, -MASK_BIG, PZ_MAX).astype(np.float32)
    diag = np.concatenate([diag, diag], axis=0)
    mask = np.stack([np.full_like(diag, PZ_MAX), diag])
    incl = np.where(row >= col, -1.0, 0.0).astype(np.float32)
    return jnp.asarray(mask), jnp.asarray(incl, dtype=BF16)


def _sb_attention(qkv, *, batch, seq, d_model):
    tile = SB_TILE
    nq = seq // tile
    pairs = SB_HEADS // 2
    qkv3 = qkv.reshape(batch, seq, 3 * d_model)
    mask, incl = _sb_constants(tile)
    kern = functools.partial(_sb_kernel, tile=tile, nq=nq)
    out = pl.pallas_call(
        kern,
        grid=(batch, pairs),
        in_specs=[
            pl.BlockSpec((None, seq, LANES), lambda b, p: (b, 0, p)),
            pl.BlockSpec((None, seq, LANES), lambda b, p: (b, 0, pairs + p)),
            pl.BlockSpec((None, seq, LANES), lambda b, p: (b, 0, 2 * pairs + p)),
            pl.BlockSpec((2, 2 * tile, tile), lambda b, p: (0, 0, 0), pipeline_mode=pl.Buffered(1)),
            pl.BlockSpec((tile, tile), lambda b, p: (0, 0), pipeline_mode=pl.Buffered(1)),
        ],
        out_specs=pl.BlockSpec((None, seq, LANES), lambda b, p: (b, 0, p)),
        out_shape=jax.ShapeDtypeStruct((batch, seq, d_model), BF16),
        scratch_shapes=[
            pltpu.VMEM((SB_SLOTS, 2 * tile, tile), F32),
            pltpu.VMEM((SB_SLOTS, 2 * tile, tile), F32),
            pltpu.VMEM((nq + 1, tile, LANES), F32),
            pltpu.VMEM((nq + 1, 2 * tile, 1), F32),
            pltpu.SMEM((2 * nq + 8,), jnp.int32),
            pltpu.SMEM((2 * nq + 8,), jnp.int32),
            pltpu.SMEM((nq + 1,), jnp.int32),
        ],
        compiler_params=_params("arbitrary", "arbitrary"),
        name="sb_attention",
    )(qkv3, qkv3, qkv3, mask, incl)
    return out.reshape(batch * seq, d_model)


def _ret_kernel(q_ref, k_ref, v_ref, g_ref, intra_ref, cross_ref, kv_ref, cd_ref, o_ref,
                state_ref, *, chunk, n_sub):
    @pl.when(pl.program_id(2) == 0)
    def _():
        state_ref[...] = jnp.zeros_like(state_ref)

    cross_d = cross_ref[...]
    kv_d = kv_ref[...]

    for c in range(n_sub):
        rs = slice(c * chunk, (c + 1) * chunk)
        qr = q_ref[rs, :]
        kr = k_ref[rs, :]
        v = v_ref[rs, :]

        scores = lax.dot_general(qr, kr, (((1,), (1,)), ((), ())),
                                 preferred_element_type=F32) * intra_ref[...]
        inner = jnp.dot(scores.astype(BF16), v, preferred_element_type=F32)

        state = state_ref[...]
        qc = (qr.astype(F32) * cross_d).astype(BF16)
        cross = jnp.dot(qc, state.astype(BF16), preferred_element_type=F32)

        kd = (kr.astype(F32) * kv_d).astype(BF16)
        state_ref[...] = state * cd_ref[...] + lax.dot_general(
            kd, v, (((0,), (0,)), ((), ())), preferred_element_type=F32)

        y = inner + cross
        y = y * lax.rsqrt(jnp.mean(y * y, axis=-1, keepdims=True) + EPS)
        o_ref[rs, :] = (g_ref[rs, :].astype(F32) * y).astype(o_ref.dtype)


def _retention_tables(seq, key_dim, val_dim, chunk):
    k_scale = key_dim ** -0.5
    inv_freq = ROPE_BASE ** (-np.arange(0, key_dim, 2, dtype=np.float64) / key_dim)
    ang = np.arange(seq, dtype=np.float64)[:, None] * inv_freq[None, :]
    log_gamma = np.log1p(-np.exp2(-5.0 - np.arange(RET_HEADS, dtype=np.float64)))
    idx = np.arange(chunk, dtype=np.float64)
    diff = idx[:, None] - idx[None, :]
    intra = np.where(diff >= 0, np.exp(log_gamma[:, None, None] * np.maximum(diff, 0.0)), 0.0)
    cross = np.exp(log_gamma[:, None] * (idx + 1.0))
    kv = np.exp(log_gamma[:, None] * (chunk - 1.0 - idx))
    cd = np.exp(log_gamma * chunk)
    tables = dict(
        cos=np.cos(ang), sin=np.sin(ang), intra=intra * k_scale,
        cross=np.broadcast_to(cross[:, :, None], (RET_HEADS, chunk, key_dim)),
        kv=np.broadcast_to((kv * k_scale)[:, :, None], (RET_HEADS, chunk, key_dim)),
        cd=np.broadcast_to(cd[:, None, None], (RET_HEADS, 1, val_dim)),
    )
    return {name: jnp.asarray(t, dtype=F32) for name, t in tables.items()}


def _retention(proj, tb, *, batch, seq, d_model):
    chunk = RET_CHUNK
    key_dim = d_model // RET_HEADS
    val_dim = 2 * d_model // RET_HEADS
    proj3 = proj.reshape(batch, seq, 6 * d_model)
    nh = RET_HEADS
    n_sub = RET_CHUNKS_PER_STEP
    blk = chunk * n_sub
    kern = functools.partial(_ret_kernel, chunk=chunk, n_sub=n_sub)
    out = pl.pallas_call(
        kern,
        grid=(batch, nh, seq // blk),
        in_specs=[
            pl.BlockSpec((None, blk, key_dim), lambda b, h, c: (b, c, h)),
            pl.BlockSpec((None, blk, key_dim), lambda b, h, c: (b, c, nh + h)),
            pl.BlockSpec((None, blk, val_dim), lambda b, h, c: (b, c, nh + h)),
            pl.BlockSpec((None, blk, val_dim), lambda b, h, c: (b, c, 2 * nh + h)),
            pl.BlockSpec((None, chunk, chunk), lambda b, h, c: (h, 0, 0)),
            pl.BlockSpec((None, chunk, key_dim), lambda b, h, c: (h, 0, 0)),
            pl.BlockSpec((None, chunk, key_dim), lambda b, h, c: (h, 0, 0)),
            pl.BlockSpec((None, 1, val_dim), lambda b, h, c: (h, 0, 0)),
        ],
        out_specs=pl.BlockSpec((None, blk, val_dim), lambda b, h, c: (b, c, h)),
        out_shape=jax.ShapeDtypeStruct((batch, seq, nh * val_dim), BF16),
        scratch_shapes=[pltpu.VMEM((key_dim, val_dim), F32)],
        compiler_params=_params("parallel", "parallel", "arbitrary"),
        name="retention",
    )(proj3, proj3, proj3, proj3, tb["intra"], tb["cross"], tb["kv"], tb["cd"])
    return out.reshape(batch * seq, nh * val_dim)


def _tail_kernel(h_ref, mix_ref, p_ref, wo_ref, gm_ref, wu_ref, wd_ref, gp_ref, wg_ref, wp_ref,
                 fg_ref, o_ref, *, final_norm, tf):
    h1 = h_ref[...] + jnp.dot(mix_ref[...], wo_ref[...], preferred_element_type=F32)
    xn = _rms_norm(h1, gm_ref[...]).astype(BF16)
    h2 = h1
    for c in range(wu_ref.shape[1] // tf):
        a = jnp.maximum(jnp.dot(xn, wu_ref[:, c * tf:(c + 1) * tf], preferred_element_type=F32), 0.0)
        h2 = h2 + jnp.dot((a * a).astype(BF16), wd_ref[c * tf:(c + 1) * tf, :],
                          preferred_element_type=F32)
    xn = _rms_norm(h2, gp_ref[...]).astype(BF16)
    gate = jax.nn.sigmoid(jnp.dot(xn, wg_ref[...], preferred_element_type=F32))
    up = jnp.dot(p_ref[...].astype(BF16), wp_ref[...], preferred_element_type=F32)
    out = h2 + up * gate
    if final_norm:
        out = _rms_norm(out, fg_ref[...])
    o_ref[...] = out


def _tail(h, mix, p, layer, w_out, g_mlp, w_up, w_down, g_ple, w_gate, w_ple, final_g, *, final_norm):
    t, d = h.shape
    dm = mix.shape[1]
    dp = p.shape[2]
    f = w_up.shape[1]
    tm = TAIL_ROW_TILE

    def whole(shape):
        return pl.BlockSpec(shape, lambda i: (0, 0), pipeline_mode=pl.Buffered(1))

    return pl.pallas_call(
        functools.partial(_tail_kernel, final_norm=final_norm, tf=MLP_FF_TILE),
        grid=(t // tm,),
        in_specs=[
            pl.BlockSpec((tm, d), lambda i: (i, 0)),
            pl.BlockSpec((tm, dm), lambda i: (i, 0)),
            pl.BlockSpec((None, tm, dp), lambda i: (layer, i, 0)),
            whole((dm, d)),
            whole((1, d)),
            whole((d, f)),
            whole((f, d)),
            whole((1, d)),
            whole((d, d)),
            whole((dp, d)),
            whole((1, d)),
        ],
        out_specs=pl.BlockSpec((tm, d), lambda i: (i, 0)),
        out_shape=jax.ShapeDtypeStruct((t, d), F32),
        compiler_params=_params("parallel"),
        name="tail",
    )(h, mix, p, w_out, g_mlp.reshape(1, d), w_up, w_down, g_ple.reshape(1, d), w_gate, w_ple,
      final_g.reshape(1, d))


def kernel(x, p, mix_norm, sb_w_in, sb_w_out, ret_w_in, ret_w_out, mlp_norm, mlp_w_up,
           mlp_w_down, ple_norm, ple_w_gate, ple_w_up, final_norm):
    batch, seq, d_model = x.shape
    depth = p.shape[0]
    t = batch * seq
    h = x.reshape(t, d_model)
    p2 = p.reshape(depth, t, p.shape[-1])
    for i in range(depth):
        if i % 2 == 0:
            q_scale = LOG2E * (d_model // SB_HEADS) ** -0.5
            col_scale = jnp.concatenate([jnp.full((d_model,), q_scale, F32), jnp.ones((2 * d_model,), F32)])
            qkv = _norm_proj(h, mix_norm[i], sb_w_in[i // 2].astype(BF16), tn=SB_PROJ_COL_TILE,
                             col_scale=col_scale)
            mix = _sb_attention(qkv, batch=batch, seq=seq, d_model=d_model)
            w_out = sb_w_out[i // 2]
        else:
            tb = _retention_tables(seq, d_model // RET_HEADS, 2 * d_model // RET_HEADS, RET_CHUNK)
            proj = _norm_proj(h, mix_norm[i], ret_w_in[i // 2].astype(BF16), tn=RET_PROJ_COL_TILE,
                              rope=(tb["cos"], tb["sin"], 2 * d_model), seq=seq,
                              silu_cols=2 * d_model)
            mix = _retention(proj, tb, batch=batch, seq=seq, d_model=d_model)
            w_out = ret_w_out[i // 2]
        h = _tail(h, mix, p2, i, w_out.astype(BF16), mlp_norm[i], mlp_w_up[i].astype(BF16),
                  mlp_w_down[i].astype(BF16), ple_norm[i], ple_w_gate[i].astype(BF16),
                  ple_w_up[i].astype(BF16), final_norm, final_norm=(i == depth - 1))
    return h.reshape(batch, seq, d_model)
```

```python
import functools
import math

import jax
import jax.numpy as jnp
import numpy as np
from jax import lax
from jax.experimental import pallas as pl
from jax.experimental.pallas import tpu as pltpu

F32 = jnp.float32
BF16 = jnp.bfloat16

EPS = 1e-6
SB_HEADS = 16
RET_HEADS = 4
ROPE_BASE = 10000.0

VMEM_LIMIT_BYTES = 56 * 1024 * 1024
LANES = 128
LOG2E = math.log2(math.e)
MASK_BIG = 1e30
PZ_MAX = 126.0
SKIP_LOG2 = -160.0

PROJ_ROW_TILE = 1024
SB_PROJ_COL_TILE = 1536
RET_PROJ_COL_TILE = 1024
TAIL_ROW_TILE = 512
MLP_FF_TILE = 1024
SB_TILE = 256
RET_CHUNK = 256
RET_CHUNKS_PER_STEP = 16


def _params(*sem):
    return pltpu.CompilerParams(dimension_semantics=sem, vmem_limit_bytes=VMEM_LIMIT_BYTES)


def _rms_norm(x, g):
    ms = jnp.mean(x * x, axis=-1, keepdims=True)
    return x * lax.rsqrt(ms + EPS) * g


def _norm_proj_kernel(*refs, tn, scaled, rope_chunks, silu_from):
    x_ref, g_ref, w_ref = refs[:3]
    o_ref = refs[-1]
    xn = _rms_norm(x_ref[...], g_ref[...]).astype(BF16)
    if rope_chunks:
        cos = refs[3][...]
        sin = refs[4][...]
        hd = cos.shape[1]
    for c in range(w_ref.shape[1] // tn):
        cols = slice(c * tn, (c + 1) * tn)
        acc = jnp.dot(xn, w_ref[:, cols], preferred_element_type=F32)
        if scaled:
            acc = acc * refs[3][:, cols]
        if silu_from is not None and c >= silu_from:
            hx = 0.5 * acc
            o_ref[:, cols] = (hx + hx * jnp.tanh(hx)).astype(o_ref.dtype)
        elif c < rope_chunks:
            for h in range(tn // (2 * hd)):
                lo = c * tn + 2 * h * hd
                x1 = acc[:, 2 * h * hd:(2 * h + 1) * hd]
                x2 = acc[:, (2 * h + 1) * hd:(2 * h + 2) * hd]
                o_ref[:, lo:lo + hd] = (x1 * cos - x2 * sin).astype(o_ref.dtype)
                o_ref[:, lo + hd:lo + 2 * hd] = (x1 * sin + x2 * cos).astype(o_ref.dtype)
        else:
            o_ref[:, cols] = acc.astype(o_ref.dtype)


def _norm_proj(h, g, w, *, tn, col_scale=None, rope=None, seq=None, silu_cols=None):
    t, d = h.shape
    n = w.shape[1]
    tm = PROJ_ROW_TILE

    def whole(shape):
        return pl.BlockSpec(shape, lambda i: (0, 0), pipeline_mode=pl.Buffered(1))

    in_specs = [pl.BlockSpec((tm, d), lambda i: (i, 0)), whole((1, d)), whole((d, n))]
    args = [h, g.reshape(1, d), w]
    rope_chunks = 0
    if col_scale is not None:
        in_specs.append(whole((1, n)))
        args.append(col_scale.reshape(1, n))
    if rope is not None:
        cos, sin, rope_cols = rope
        rope_chunks = rope_cols // tn
        tiles_per_seq = seq // tm
        for tbl in (cos, sin):
            in_specs.append(pl.BlockSpec((tm, tbl.shape[1]), lambda i: (i % tiles_per_seq, 0)))
            args.append(tbl)
    return pl.pallas_call(
        functools.partial(_norm_proj_kernel, tn=tn, scaled=col_scale is not None,
                          rope_chunks=rope_chunks,
                          silu_from=None if silu_cols is None else (n - silu_cols) // tn),
        grid=(t // tm,),
        in_specs=in_specs,
        out_specs=pl.BlockSpec((tm, n), lambda i: (i, 0)),
        out_shape=jax.ShapeDtypeStruct((t, n), BF16),
        compiler_params=_params("parallel"),
        name="norm_proj",
    )(*args)


def _sb_kernel(q_ref, k_ref, v_ref, mask_ref, incl_ref, o_ref, pz_buf, sfx_buf, acc_ref,
               carry_ref, lst_q, lst_k, done_ref, *, tile, nq):
    half = LANES // 2
    lane = lax.broadcasted_iota(jnp.int32, (1, LANES), 1)
    head0 = lane < half

    @pl.when((pl.program_id(0) == 0) & (pl.program_id(1) == 0))
    def _():
        pz_buf[...] = jnp.zeros_like(pz_buf)
        sfx_buf[...] = jnp.zeros_like(sfx_buf)
        acc_ref[...] = jnp.zeros_like(acc_ref)
        carry_ref[...] = jnp.zeros_like(carry_ref)

    for i in range(nq + 1):
        done_ref[i] = 0

    def rows(idx):
        if isinstance(idx, int):
            return slice(idx * tile, (idx + 1) * tile)
        return pl.ds(pl.multiple_of(idx * tile, tile), tile)


    def stage_z(st, kj, slot, fresh, flag):
        suffix = sfx_buf[slot]
        if fresh is True:
            carry = suffix[:, 0:1]
            a = jnp.exp2(suffix + pz_buf[slot])
        else:
            carry = carry_ref[st] if fresh is False else jnp.where(fresh, 0.0, carry_ref[st])
            a = jnp.exp2(suffix + pz_buf[slot] + carry)
            carry = carry + suffix[:, 0:1]
        carry_ref[st] = carry
        if flag:
            done_ref[st] = (jnp.max(carry) < SKIP_LOG2).astype(jnp.int32)
        res = jnp.dot(a.astype(BF16), v_ref[rows(kj), :], preferred_element_type=F32)
        res = jnp.where(head0, res[:tile], res[tile:])
        if fresh is True:
            acc_ref[st] = res
        elif fresh is False:
            acc_ref[st] = res + acc_ref[st]
        else:
            acc_ref[st] = res + jnp.where(fresh, 0.0, acc_ref[st])

    def stage_y(slot):
        l = jnp.log(1.0 + jnp.exp2(pz_buf[slot])) * LOG2E
        sfx_buf[slot] = jnp.dot(l.astype(BF16), incl_ref[...], preferred_element_type=F32)

    def stage_x(qi, kj, slot, diag):
        q = q_ref[rows(qi), :]
        qzero = jnp.zeros_like(q)
        q2 = jnp.concatenate([jnp.where(head0, q, qzero), jnp.where(head0, qzero, q)], axis=0)
        pz_raw = lax.dot_general(q2, k_ref[rows(kj), :], (((1,), (1,)), ((), ())),
                                 preferred_element_type=F32)
        pz_buf[slot] = jnp.minimum(pz_raw, mask_ref[diag])

    first = [(qi, qi) for qi in range(nq)] + [(qi, qi - 1) for qi in range(1, nq)]
    for s in range(len(first) + 2):
        if s >= 2:
            qi, kj = first[s - 2]
            stage_z(qi, kj, s % 2, qi == kj, qi != kj)
        if 1 <= s <= len(first):
            stage_y((s - 1) % 2)
        if s < len(first):
            qi, kj = first[s]
            stage_x(qi, kj, s % 2, int(qi == kj))

    def entry(e):
        st = lst_q[e]
        return st, jnp.minimum(st, nq - 1), lst_k[e]

    def sub_step(e, slot):
        st, _, kj = entry(e)
        stage_z(st, kj, slot, st == nq, True)
        stage_y(1 - slot)
        _, qi, kj = entry(e + 2)
        stage_x(qi, kj, slot, 0)

    def put(i, st, kj):
        lst_q[i] = st
        lst_k[i] = kj

    def offset_body(state):
        d, _ = state
        put(0, nq, 0)
        put(1, nq, 0)
        n = 0
        for off in range(2):
            def add(qi, n, off=off):
                put(2 + n, qi, qi - d - off)
                return n + (done_ref[qi] == 0).astype(jnp.int32)

            n = lax.fori_loop(d + off, nq, add, n)
        for t in range(4):
            put(2 + n + t, nq, 0)

        def body(i, c):
            sub_step(2 * i, 0)
            sub_step(2 * i + 1, 1)
            return c

        lax.fori_loop(0, jnp.where(n > 0, (n + 3) // 2, 0), body, 0)
        return d + 2, n

    lax.while_loop(lambda st: (st[0] < nq) & (st[1] > 0), offset_body, (2, 1))

    for qi in range(nq):
        o_ref[qi * tile:(qi + 1) * tile, :] = acc_ref[qi].astype(o_ref.dtype)


def _sb_constants(tile):
    row = np.arange(tile)[:, None]
    col = np.arange(tile)[None, :]
    diag = np.where(col >= row, -MASK_BIG, PZ_MAX).astype(np.float32)
    diag = np.concatenate([diag, diag], axis=0)
    mask = np.stack([np.full_like(diag, PZ_MAX), diag])
    incl = np.where(row >= col, -1.0, 0.0).astype(np.float32)
    return jnp.asarray(mask), jnp.asarray(incl, dtype=BF16)


def _sb_attention(qkv, *, batch, seq, d_model):
    tile = SB_TILE
    nq = seq // tile
    pairs = SB_HEADS // 2
    qkv3 = qkv.reshape(batch, seq, 3 * d_model)
    mask, incl = _sb_constants(tile)
    kern = functools.partial(_sb_kernel, tile=tile, nq=nq)
    out = pl.pallas_call(
        kern,
        grid=(batch, pairs),
        in_specs=[
            pl.BlockSpec((None, seq, LANES), lambda b, p: (b, 0, p)),
            pl.BlockSpec((None, seq, LANES), lambda b, p: (b, 0, pairs + p)),
            pl.BlockSpec((None, seq, LANES), lambda b, p: (b, 0, 2 * pairs + p)),
            pl.BlockSpec((2, 2 * tile, tile), lambda b, p: (0, 0, 0), pipeline_mode=pl.Buffered(1)),
            pl.BlockSpec((tile, tile), lambda b, p: (0, 0), pipeline_mode=pl.Buffered(1)),
        ],
        out_specs=pl.BlockSpec((None, seq, LANES), lambda b, p: (b, 0, p)),
        out_shape=jax.ShapeDtypeStruct((batch, seq, d_model), BF16),
        scratch_shapes=[
            pltpu.VMEM((2, 2 * tile, tile), F32),
            pltpu.VMEM((2, 2 * tile, tile), F32),
            pltpu.VMEM((nq + 1, tile, LANES), F32),
            pltpu.VMEM((nq + 1, 2 * tile, 1), F32),
            pltpu.SMEM((2 * nq + 8,), jnp.int32),
            pltpu.SMEM((2 * nq + 8,), jnp.int32),
            pltpu.SMEM((nq + 1,), jnp.int32),
        ],
        compiler_params=_params("arbitrary", "arbitrary"),
        name="sb_attention",
    )(qkv3, qkv3, qkv3, mask, incl)
    return out.reshape(batch * seq, d_model)


def _ret_kernel(q_ref, k_ref, v_ref, g_ref, intra_ref, cross_ref, kv_ref, cd_ref, o_ref,
                state_ref, *, chunk, n_sub):
    @pl.when(pl.program_id(2) == 0)
    def _():
        state_ref[...] = jnp.zeros_like(state_ref)

    cross_d = cross_ref[...]
    kv_d = kv_ref[...]

    for c in range(n_sub):
        rs = slice(c * chunk, (c + 1) * chunk)
        qr = q_ref[rs, :]
        kr = k_ref[rs, :]
        v = v_ref[rs, :]

        scores = lax.dot_general(qr, kr, (((1,), (1,)), ((), ())),
                                 preferred_element_type=F32) * intra_ref[...]
        inner = jnp.dot(scores.astype(BF16), v, preferred_element_type=F32)

        state = state_ref[...]
        qc = (qr.astype(F32) * cross_d).astype(BF16)
        cross = jnp.dot(qc, state.astype(BF16), preferred_element_type=F32)

        kd = (kr.astype(F32) * kv_d).astype(BF16)
        state_ref[...] = state * cd_ref[...] + lax.dot_general(
            kd, v, (((0,), (0,)), ((), ())), preferred_element_type=F32)

        y = inner + cross
        y = y * lax.rsqrt(jnp.mean(y * y, axis=-1, keepdims=True) + EPS)
        o_ref[rs, :] = (g_ref[rs, :].astype(F32) * y).astype(o_ref.dtype)


def _retention_tables(seq, key_dim, val_dim, chunk):
    k_scale = key_dim ** -0.5
    inv_freq = ROPE_BASE ** (-np.arange(0, key_dim, 2, dtype=np.float64) / key_dim)
    ang = np.arange(seq, dtype=np.float64)[:, None] * inv_freq[None, :]
    log_gamma = np.log1p(-np.exp2(-5.0 - np.arange(RET_HEADS, dtype=np.float64)))
    idx = np.arange(chunk, dtype=np.float64)
    diff = idx[:, None] - idx[None, :]
    intra = np.where(diff >= 0, np.exp(log_gamma[:, None, None] * np.maximum(diff, 0.0)), 0.0)
    cross = np.exp(log_gamma[:, None] * (idx + 1.0))
    kv = np.exp(log_gamma[:, None] * (chunk - 1.0 - idx))
    cd = np.exp(log_gamma * chunk)
    tables = dict(
        cos=np.cos(ang), sin=np.sin(ang), intra=intra * k_scale,
        cross=np.broadcast_to(cross[:, :, None], (RET_HEADS, chunk, key_dim)),
        kv=np.broadcast_to((kv * k_scale)[:, :, None], (RET_HEADS, chunk, key_dim)),
        cd=np.broadcast_to(cd[:, None, None], (RET_HEADS, 1, val_dim)),
    )
    return {name: jnp.asarray(t, dtype=F32) for name, t in tables.items()}


def _retention(proj, tb, *, batch, seq, d_model):
    chunk = RET_CHUNK
    key_dim = d_model // RET_HEADS
    val_dim = 2 * d_model // RET_HEADS
    proj3 = proj.reshape(batch, seq, 6 * d_model)
    nh = RET_HEADS
    n_sub = RET_CHUNKS_PER_STEP
    blk = chunk * n_sub
    kern = functools.partial(_ret_kernel, chunk=chunk, n_sub=n_sub)
    out = pl.pallas_call(
        kern,
        grid=(batch, nh, seq // blk),
        in_specs=[
            pl.BlockSpec((None, blk, key_dim), lambda b, h, c: (b, c, h)),
            pl.BlockSpec((None, blk, key_dim), lambda b, h, c: (b, c, nh + h)),
            pl.BlockSpec((None, blk, val_dim), lambda b, h, c: (b, c, nh + h)),
            pl.BlockSpec((None, blk, val_dim), lambda b, h, c: (b, c, 2 * nh + h)),
            pl.BlockSpec((None, chunk, chunk), lambda b, h, c: (h, 0, 0)),
            pl.BlockSpec((None, chunk, key_dim), lambda b, h, c: (h, 0, 0)),
            pl.BlockSpec((None, chunk, key_dim), lambda b, h, c: (h, 0, 0)),
            pl.BlockSpec((None, 1, val_dim), lambda b, h, c: (h, 0, 0)),
        ],
        out_specs=pl.BlockSpec((None, blk, val_dim), lambda b, h, c: (b, c, h)),
        out_shape=jax.ShapeDtypeStruct((batch, seq, nh * val_dim), BF16),
        scratch_shapes=[pltpu.VMEM((key_dim, val_dim), F32)],
        compiler_params=_params("parallel", "parallel", "arbitrary"),
        name="retention",
    )(proj3, proj3, proj3, proj3, tb["intra"], tb["cross"], tb["kv"], tb["cd"])
    return out.reshape(batch * seq, nh * val_dim)


def _tail_kernel(h_ref, mix_ref, p_ref, wo_ref, gm_ref, wu_ref, wd_ref, gp_ref, wg_ref, wp_ref,
                 fg_ref, o_ref, *, final_norm, tf):
    h1 = h_ref[...] + jnp.dot(mix_ref[...], wo_ref[...], preferred_element_type=F32)
    xn = _rms_norm(h1, gm_ref[...]).astype(BF16)
    h2 = h1
    for c in range(wu_ref.shape[1] // tf):
        a = jnp.maximum(jnp.dot(xn, wu_ref[:, c * tf:(c + 1) * tf], preferred_element_type=F32), 0.0)
        h2 = h2 + jnp.dot((a * a).astype(BF16), wd_ref[c * tf:(c + 1) * tf, :],
                          preferred_element_type=F32)
    xn = _rms_norm(h2, gp_ref[...]).astype(BF16)
    gate = jax.nn.sigmoid(jnp.dot(xn, wg_ref[...], preferred_element_type=F32))
    up = jnp.dot(p_ref[...].astype(BF16), wp_ref[...], preferred_element_type=F32)
    out = h2 + up * gate
    if final_norm:
        out = _rms_norm(out, fg_ref[...])
    o_ref[...] = out


def _tail(h, mix, p, layer, w_out, g_mlp, w_up, w_down, g_ple, w_gate, w_ple, final_g, *, final_norm):
    t, d = h.shape
    dm = mix.shape[1]
    dp = p.shape[2]
    f = w_up.shape[1]
    tm = TAIL_ROW_TILE

    def whole(shape):
        return pl.BlockSpec(shape, lambda i: (0, 0), pipeline_mode=pl.Buffered(1))

    return pl.pallas_call(
        functools.partial(_tail_kernel, final_norm=final_norm, tf=MLP_FF_TILE),
        grid=(t // tm,),
        in_specs=[
            pl.BlockSpec((tm, d), lambda i: (i, 0)),
            pl.BlockSpec((tm, dm), lambda i: (i, 0)),
            pl.BlockSpec((None, tm, dp), lambda i: (layer, i, 0)),
            whole((dm, d)),
            whole((1, d)),
            whole((d, f)),
            whole((f, d)),
            whole((1, d)),
            whole((d, d)),
            whole((dp, d)),
            whole((1, d)),
        ],
        out_specs=pl.BlockSpec((tm, d), lambda i: (i, 0)),
        out_shape=jax.ShapeDtypeStruct((t, d), F32),
        compiler_params=_params("parallel"),
        name="tail",
    )(h, mix, p, w_out, g_mlp.reshape(1, d), w_up, w_down, g_ple.reshape(1, d), w_gate, w_ple,
      final_g.reshape(1, d))


def kernel(x, p, mix_norm, sb_w_in, sb_w_out, ret_w_in, ret_w_out, mlp_norm, mlp_w_up,
           mlp_w_down, ple_norm, ple_w_gate, ple_w_up, final_norm):
    batch, seq, d_model = x.shape
    depth = p.shape[0]
    t = batch * seq
    h = x.reshape(t, d_model)
    p2 = p.reshape(depth, t, p.shape[-1])
    for i in range(depth):
        if i % 2 == 0:
            q_scale = LOG2E * (d_model // SB_HEADS) ** -0.5
            col_scale = jnp.concatenate([jnp.full((d_model,), q_scale, F32), jnp.ones((2 * d_model,), F32)])
            qkv = _norm_proj(h, mix_norm[i], sb_w_in[i // 2].astype(BF16), tn=SB_PROJ_COL_TILE,
                             col_scale=col_scale)
            mix = _sb_attention(qkv, batch=batch, seq=seq, d_model=d_model)
            w_out = sb_w_out[i // 2]
        else:
            tb = _retention_tables(seq, d_model // RET_HEADS, 2 * d_model // RET_HEADS, RET_CHUNK)
            proj = _norm_proj(h, mix_norm[i], ret_w_in[i // 2].astype(BF16), tn=RET_PROJ_COL_TILE,
                              rope=(tb["cos"], tb["sin"], 2 * d_model), seq=seq,
                              silu_cols=2 * d_model)
            mix = _retention(proj, tb, batch=batch, seq=seq, d_model=d_model)
            w_out = ret_w_out[i // 2]
        h = _tail(h, mix, p2, i, w_out.astype(BF16), mlp_norm[i], mlp_w_up[i].astype(BF16),
                  mlp_w_down[i].astype(BF16), ple_norm[i], ple_w_gate[i].astype(BF16),
                  ple_w_up[i].astype(BF16), final_norm, final_norm=(i == depth - 1))
    return h.reshape(batch, seq, d_model)
```

```python
import functools
import math

import jax
import jax.numpy as jnp
import numpy as np
from jax import lax
from jax.experimental import pallas as pl
from jax.experimental.pallas import tpu as pltpu

F32 = jnp.float32
BF16 = jnp.bfloat16

EPS = 1e-6
SB_HEADS = 16
RET_HEADS = 4
ROPE_BASE = 10000.0

VMEM_LIMIT_BYTES = 56 * 1024 * 1024
LANES = 128
LOG2E = math.log2(math.e)
MASK_BIG = 1e30
PZ_MAX = 126.0
SKIP_LOG2 = -160.0

PROJ_ROW_TILE = 1024
SB_PROJ_COL_TILE = 1536
RET_PROJ_COL_TILE = 1024
TAIL_ROW_TILE = 512
MLP_FF_TILE = 1024
SB_TILE = 256
RET_CHUNK = 256
RET_CHUNKS_PER_STEP = 16


def _params(*sem):
    return pltpu.CompilerParams(dimension_semantics=sem, vmem_limit_bytes=VMEM_LIMIT_BYTES)


def _rms_norm(x, g):
    ms = jnp.mean(x * x, axis=-1, keepdims=True)
    return x * lax.rsqrt(ms + EPS) * g


def _norm_proj_kernel(*refs, tn, scaled, rope_chunks, silu_from):
    x_ref, g_ref, w_ref = refs[:3]
    o_ref = refs[-1]
    xn = _rms_norm(x_ref[...], g_ref[...]).astype(BF16)
    if rope_chunks:
        cos = refs[3][...]
        sin = refs[4][...]
        hd = cos.shape[1]
    for c in range(w_ref.shape[1] // tn):
        cols = slice(c * tn, (c + 1) * tn)
        acc = jnp.dot(xn, w_ref[:, cols], preferred_element_type=F32)
        if scaled:
            acc = acc * refs[3][:, cols]
        if silu_from is not None and c >= silu_from:
            hx = 0.5 * acc
            o_ref[:, cols] = (hx + hx * jnp.tanh(hx)).astype(o_ref.dtype)
        elif c < rope_chunks:
            for h in range(tn // (2 * hd)):
                lo = c * tn + 2 * h * hd
                x1 = acc[:, 2 * h * hd:(2 * h + 1) * hd]
                x2 = acc[:, (2 * h + 1) * hd:(2 * h + 2) * hd]
                o_ref[:, lo:lo + hd] = (x1 * cos - x2 * sin).astype(o_ref.dtype)
                o_ref[:, lo + hd:lo + 2 * hd] = (x1 * sin + x2 * cos).astype(o_ref.dtype)
        else:
            o_ref[:, cols] = acc.astype(o_ref.dtype)


def _norm_proj(h, g, w, *, tn, col_scale=None, rope=None, seq=None, silu_cols=None):
    t, d = h.shape
    n = w.shape[1]
    tm = PROJ_ROW_TILE

    def whole(shape):
        return pl.BlockSpec(shape, lambda i: (0, 0), pipeline_mode=pl.Buffered(1))

    in_specs = [pl.BlockSpec((tm, d), lambda i: (i, 0)), whole((1, d)), whole((d, n))]
    args = [h, g.reshape(1, d), w]
    rope_chunks = 0
    if col_scale is not None:
        in_specs.append(whole((1, n)))
        args.append(col_scale.reshape(1, n))
    if rope is not None:
        cos, sin, rope_cols = rope
        rope_chunks = rope_cols // tn
        tiles_per_seq = seq // tm
        for tbl in (cos, sin):
            in_specs.append(pl.BlockSpec((tm, tbl.shape[1]), lambda i: (i % tiles_per_seq, 0)))
            args.append(tbl)
    return pl.pallas_call(
        functools.partial(_norm_proj_kernel, tn=tn, scaled=col_scale is not None,
                          rope_chunks=rope_chunks,
                          silu_from=None if silu_cols is None else (n - silu_cols) // tn),
        grid=(t // tm,),
        in_specs=in_specs,
        out_specs=pl.BlockSpec((tm, n), lambda i: (i, 0)),
        out_shape=jax.ShapeDtypeStruct((t, n), BF16),
        compiler_params=_params("parallel"),
        name="norm_proj",
    )(*args)


def _sb_kernel(q_ref, k_ref, v_ref, mask_ref, incl_ref, o_ref, pz_buf, sfx_buf, acc_ref,
               carry_ref, lst_q, lst_k, done_ref, *, tile, nq):
    half = LANES // 2
    lane = lax.broadcasted_iota(jnp.int32, (1, LANES), 1)
    head0 = lane < half

    @pl.when((pl.program_id(0) == 0) & (pl.program_id(1) == 0))
    def _():
        pz_buf[...] = jnp.zeros_like(pz_buf)
        sfx_buf[...] = jnp.zeros_like(sfx_buf)
        acc_ref[...] = jnp.zeros_like(acc_ref)
        carry_ref[...] = jnp.zeros_like(carry_ref)

    for i in range(nq + 1):
        done_ref[i] = 0

    def rows(idx):
        if isinstance(idx, int):
            return slice(idx * tile, (idx + 1) * tile)
        return pl.ds(pl.multiple_of(idx * tile, tile), tile)


    def stage_z(st, kj, slot, fresh, flag):
        suffix = sfx_buf[slot]
        if fresh is True:
            carry = suffix[:, 0:1]
            a = jnp.exp2(suffix + pz_buf[slot])
        else:
            carry = carry_ref[st] if fresh is False else jnp.where(fresh, 0.0, carry_ref[st])
            a = jnp.exp2(suffix + pz_buf[slot] + carry)
            carry = carry + suffix[:, 0:1]
        carry_ref[st] = carry
        if flag:
            done_ref[st] = (jnp.max(carry) < SKIP_LOG2).astype(jnp.int32)
        res = jnp.dot(a.astype(BF16), v_ref[rows(kj), :], preferred_element_type=F32)
        res = jnp.where(head0, res[:tile], res[tile:])
        if fresh is True:
            acc_ref[st] = res
        elif fresh is False:
            acc_ref[st] = res + acc_ref[st]
        else:
            acc_ref[st] = res + jnp.where(fresh, 0.0, acc_ref[st])

    def stage_y(slot):
        l = jnp.log(1.0 + jnp.exp2(pz_buf[slot])) * LOG2E
        sfx_buf[slot] = jnp.dot(l.astype(BF16), incl_ref[...], preferred_element_type=F32)

    def stage_x(qi, kj, slot, diag):
        q = q_ref[rows(qi), :]
        qzero = jnp.zeros_like(q)
        q2 = jnp.concatenate([jnp.where(head0, q, qzero), jnp.where(head0, qzero, q)], axis=0)
        pz_raw = lax.dot_general(q2, k_ref[rows(kj), :], (((1,), (1,)), ((), ())),
                                 preferred_element_type=F32)
        pz_buf[slot] = jnp.minimum(pz_raw, mask_ref[diag])

    first = [(qi, qi) for qi in range(nq)] + [(qi, qi - 1) for qi in range(1, nq)]
    for s in range(len(first) + 2):
        if s >= 2:
            qi, kj = first[s - 2]
            stage_z(qi, kj, s % 2, qi == kj, qi != kj)
        if 1 <= s <= len(first):
            stage_y((s - 1) % 2)
        if s < len(first):
            qi, kj = first[s]
            stage_x(qi, kj, s % 2, int(qi == kj))

    def entry(e):
        st = lst_q[e]
        return st, jnp.minimum(st, nq - 1), lst_k[e]

    def sub_step(e, slot):
        st, _, kj = entry(e)
        stage_z(st, kj, slot, st == nq, True)
        stage_y(1 - slot)
        _, qi, kj = entry(e + 2)
        stage_x(qi, kj, slot, 0)

    def put(i, st, kj):
        lst_q[i] = st
        lst_k[i] = kj

    def offset_body(state):
        d, _ = state
        put(0, nq, 0)
        put(1, nq, 0)
        n = 0
        for off in range(2):
            def add(qi, n, off=off):
                put(2 + n, qi, qi - d - off)
                return n + (done_ref[qi] == 0).astype(jnp.int32)

            n = lax.fori_loop(d + off, nq, add, n)
        for t in range(4):
            put(2 + n + t, nq, 0)

        def body(i, c):
            sub_step(2 * i, 0)
            sub_step(2 * i + 1, 1)
            return c

        lax.fori_loop(0, jnp.where(n > 0, (n + 3) // 2, 0), body, 0)
        return d + 2, n

    lax.while_loop(lambda st: (st[0] < nq) & (st[1] > 0), offset_body, (2, 1))

    for qi in range(nq):
        o_ref[qi * tile:(qi + 1) * tile, :] = acc_ref[qi].astype(o_ref.dtype)


def _sb_constants(tile):
    row = np.arange(tile)[:, None]
    col = np.arange(tile)[None, :]
    diag = np.where(col >= row, -MASK_BIG, PZ_MAX).astype(np.float32)
    diag = np.concatenate([diag, diag], axis=0)
    mask = np.stack([np.full_like(diag, PZ_MAX), diag])
    incl = np.where(row >= col, -1.0, 0.0).astype(np.float32)
    return jnp.asarray(mask), jnp.asarray(incl, dtype=BF16)


def _sb_attention(qkv, *, batch, seq, d_model):
    tile = SB_TILE
    nq = seq // tile
    pairs = SB_HEADS // 2
    qkv3 = qkv.reshape(batch, seq, 3 * d_model)
    mask, incl = _sb_constants(tile)
    kern = functools.partial(_sb_kernel, tile=tile, nq=nq)
    out = pl.pallas_call(
        kern,
        grid=(batch, pairs),
        in_specs=[
            pl.BlockSpec((None, seq, LANES), lambda b, p: (b, 0, p)),
            pl.BlockSpec((None, seq, LANES), lambda b, p: (b, 0, pairs + p)),
            pl.BlockSpec((None, seq, LANES), lambda b, p: (b, 0, 2 * pairs + p)),
            pl.BlockSpec((2, 2 * tile, tile), lambda b, p: (0, 0, 0), pipeline_mode=pl.Buffered(1)),
            pl.BlockSpec((tile, tile), lambda b, p: (0, 0), pipeline_mode=pl.Buffered(1)),
        ],
        out_specs=pl.BlockSpec((None, seq, LANES), lambda b, p: (b, 0, p)),
        out_shape=jax.ShapeDtypeStruct((batch, seq, d_model), BF16),
        scratch_shapes=[
            pltpu.VMEM((2, 2 * tile, tile), F32),
            pltpu.VMEM((2, 2 * tile, tile), F32),
            pltpu.VMEM((nq + 1, tile, LANES), F32),
            pltpu.VMEM((nq + 1, 2 * tile, 1), F32),
            pltpu.SMEM((2 * nq + 8,), jnp.int32),
            pltpu.SMEM((2 * nq + 8,), jnp.int32),
            pltpu.SMEM((nq + 1,), jnp.int32),
        ],
        compiler_params=_params("arbitrary", "arbitrary"),
        name="sb_attention",
    )(qkv3, qkv3, qkv3, mask, incl)
    return out.reshape(batch * seq, d_model)


def _ret_kernel(q_ref, k_ref, v_ref, g_ref, intra_ref, cross_ref, kv_ref, cd_ref, o_ref,
                state_ref, *, chunk, n_sub):
    @pl.when(pl.program_id(2) == 0)
    def _():
        state_ref[...] = jnp.zeros_like(state_ref)

    cross_d = cross_ref[...]
    kv_d = kv_ref[...]

    for c in range(n_sub):
        rs = slice(c * chunk, (c + 1) * chunk)
        qr = q_ref[rs, :]
        kr = k_ref[rs, :]
        v = v_ref[rs, :]

        scores = lax.dot_general(qr, kr, (((1,), (1,)), ((), ())),
                                 preferred_element_type=F32) * intra_ref[...]
        inner = jnp.dot(scores.astype(BF16), v, preferred_element_type=F32)

        state = state_ref[...]
        qc = (qr.astype(F32) * cross_d).astype(BF16)
        cross = jnp.dot(qc, state.astype(BF16), preferred_element_type=F32)

        kd = (kr.astype(F32) * kv_d).astype(BF16)
        state_ref[...] = state * cd_ref[...] + lax.dot_general(
            kd, v, (((0,), (0,)), ((), ())), preferred_element_type=F32)

        y = inner + cross
        y = y * lax.rsqrt(jnp.mean(y * y, axis=-1, keepdims=True) + EPS)
        o_ref[rs, :] = (g_ref[rs, :].astype(F32) * y).astype(o_ref.dtype)


def _retention_tables(seq, key_dim, val_dim, chunk):
    k_scale = key_dim ** -0.5
    inv_freq = ROPE_BASE ** (-np.arange(0, key_dim, 2, dtype=np.float64) / key_dim)
    ang = np.arange(seq, dtype=np.float64)[:, None] * inv_freq[None, :]
    log_gamma = np.log1p(-np.exp2(-5.0 - np.arange(RET_HEADS, dtype=np.float64)))
    idx = np.arange(chunk, dtype=np.float64)
    diff = idx[:, None] - idx[None, :]
    intra = np.where(diff >= 0, np.exp(log_gamma[:, None, None] * np.maximum(diff, 0.0)), 0.0)
    cross = np.exp(log_gamma[:, None] * (idx + 1.0))
    kv = np.exp(log_gamma[:, None] * (chunk - 1.0 - idx))
    cd = np.exp(log_gamma * chunk)
    tables = dict(
        cos=np.cos(ang), sin=np.sin(ang), intra=intra * k_scale,
        cross=np.broadcast_to(cross[:, :, None], (RET_HEADS, chunk, key_dim)),
        kv=np.broadcast_to((kv * k_scale)[:, :, None], (RET_HEADS, chunk, key_dim)),
        cd=np.broadcast_to(cd[:, None, None], (RET_HEADS, 1, val_dim)),
    )
    return {name: jnp.asarray(t, dtype=F32) for name, t in tables.items()}


def _retention(proj, tb, *, batch, seq, d_model):
    chunk = RET_CHUNK
    key_dim = d_model // RET_HEADS
    val_dim = 2 * d_model // RET_HEADS
    proj3 = proj.reshape(batch, seq, 6 * d_model)
    nh = RET_HEADS
    n_sub = RET_CHUNKS_PER_STEP
    blk = chunk * n_sub
    kern = functools.partial(_ret_kernel, chunk=chunk, n_sub=n_sub)
    out = pl.pallas_call(
        kern,
        grid=(batch, nh, seq // blk),
        in_specs=[
            pl.BlockSpec((None, blk, key_dim), lambda b, h, c: (b, c, h)),
            pl.BlockSpec((None, blk, key_dim), lambda b, h, c: (b, c, nh + h)),
            pl.BlockSpec((None, blk, val_dim), lambda b, h, c: (b, c, nh + h)),
            pl.BlockSpec((None, blk, val_dim), lambda b, h, c: (b, c, 2 * nh + h)),
            pl.BlockSpec((None, chunk, chunk), lambda b, h, c: (h, 0, 0)),
            pl.BlockSpec((None, chunk, key_dim), lambda b, h, c: (h, 0, 0)),
            pl.BlockSpec((None, chunk, key_dim), lambda b, h, c: (h, 0, 0)),
            pl.BlockSpec((None, 1, val_dim), lambda b, h, c: (h, 0, 0)),
        ],
        out_specs=pl.BlockSpec((None, blk, val_dim), lambda b, h, c: (b, c, h)),
        out_shape=jax.ShapeDtypeStruct((batch, seq, nh * val_dim), BF16),
        scratch_shapes=[pltpu.VMEM((key_dim, val_dim), F32)],
        compiler_params=_params("parallel", "parallel", "arbitrary"),
        name="retention",
    )(proj3, proj3, proj3, proj3, tb["intra"], tb["cross"], tb["kv"], tb["cd"])
    return out.reshape(batch * seq, nh * val_dim)


def _tail_kernel(*refs, final_norm, tf, n_cast):
    (h_ref, mix_ref, p_ref, wo_ref, gm_ref, wu_ref, wd_ref, gp_ref, wg_ref, wp_ref,
     fg_ref) = refs[:11]
    cast_in = refs[11:11 + n_cast]
    o_ref = refs[11 + n_cast]
    cast_out = refs[12 + n_cast:]

    h1 = h_ref[...] + jnp.dot(mix_ref[...], wo_ref[...], preferred_element_type=F32)
    xn = _rms_norm(h1, gm_ref[...]).astype(BF16)
    h2 = h1
    for c in range(wu_ref.shape[1] // tf):
        a = jnp.maximum(jnp.dot(xn, wu_ref[:, c * tf:(c + 1) * tf], preferred_element_type=F32), 0.0)
        h2 = h2 + jnp.dot((a * a).astype(BF16), wd_ref[c * tf:(c + 1) * tf, :],
                          preferred_element_type=F32)
    xn = _rms_norm(h2, gp_ref[...]).astype(BF16)
    gate = jax.nn.sigmoid(jnp.dot(xn, wg_ref[...], preferred_element_type=F32))
    up = jnp.dot(p_ref[...].astype(BF16), wp_ref[...], preferred_element_type=F32)
    out = h2 + up * gate
    if final_norm:
        out = _rms_norm(out, fg_ref[...])
    o_ref[...] = out

    for src, dst in zip(cast_in, cast_out):
        dst[...] = src[...].astype(dst.dtype)


def _tail(h, mix, p, layer, w_out, g_mlp, w_up, w_down, g_ple, w_gate, w_ple, final_g, *,
          final_norm, cast=()):
    t, d = h.shape
    dm = mix.shape[1]
    dp = p.shape[2]
    f = w_up.shape[1]
    tm = TAIL_ROW_TILE
    steps = t // tm

    def whole(shape):
        return pl.BlockSpec(shape, lambda i: (0, 0), pipeline_mode=pl.Buffered(1))

    cast_specs = [pl.BlockSpec((None, w.shape[1] // steps, w.shape[2]), lambda i, k=k: (k, i, 0))
                  for w, k in cast]
    cast_out_specs = [pl.BlockSpec((w.shape[1] // steps, w.shape[2]), lambda i: (i, 0))
                      for w, _ in cast]
    outs = pl.pallas_call(
        functools.partial(_tail_kernel, final_norm=final_norm, tf=MLP_FF_TILE, n_cast=len(cast)),
        grid=(steps,),
        in_specs=[
            pl.BlockSpec((tm, d), lambda i: (i, 0)),
            pl.BlockSpec((tm, dm), lambda i: (i, 0)),
            pl.BlockSpec((None, tm, dp), lambda i: (layer, i, 0)),
            whole((dm, d)),
            whole((1, d)),
            whole((d, f)),
            whole((f, d)),
            whole((1, d)),
            whole((d, d)),
            whole((dp, d)),
            whole((1, d)),
        ] + cast_specs,
        out_specs=[pl.BlockSpec((tm, d), lambda i: (i, 0))] + cast_out_specs,
        out_shape=[jax.ShapeDtypeStruct((t, d), F32)]
        + [jax.ShapeDtypeStruct(w.shape[1:], BF16) for w, _ in cast],
        compiler_params=_params("parallel"),
        name="tail",
    )(h, mix, p, w_out, g_mlp.reshape(1, d), w_up, w_down, g_ple.reshape(1, d), w_gate, w_ple,
      final_g.reshape(1, d), *[w for w, _ in cast])
    return outs[0], outs[1:]


def kernel(x, p, mix_norm, sb_w_in, sb_w_out, ret_w_in, ret_w_out, mlp_norm, mlp_w_up,
           mlp_w_down, ple_norm, ple_w_gate, ple_w_up, final_norm):
    batch, seq, d_model = x.shape
    depth = p.shape[0]
    t = batch * seq
    h = x.reshape(t, d_model)
    p2 = p.reshape(depth, t, p.shape[-1])

    def layer_weights(i):
        mixer = (sb_w_in, sb_w_out) if i % 2 == 0 else (ret_w_in, ret_w_out)
        return [(mixer[0], i // 2), (mixer[1], i // 2), (mlp_w_up, i), (mlp_w_down, i),
                (ple_w_gate, i)]

    w_in, w_out, w_up, w_down, w_gate = [w[k].astype(BF16) for w, k in layer_weights(0)]
    for i in range(depth):
        if i % 2 == 0:
            q_scale = LOG2E * (d_model // SB_HEADS) ** -0.5
            col_scale = jnp.concatenate([jnp.full((d_model,), q_scale, F32), jnp.ones((2 * d_model,), F32)])
            qkv = _norm_proj(h, mix_norm[i], w_in, tn=SB_PROJ_COL_TILE, col_scale=col_scale)
            mix = _sb_attention(qkv, batch=batch, seq=seq, d_model=d_model)
        else:
            tb = _retention_tables(seq, d_model // RET_HEADS, 2 * d_model // RET_HEADS, RET_CHUNK)
            proj = _norm_proj(h, mix_norm[i], w_in, tn=RET_PROJ_COL_TILE,
                              rope=(tb["cos"], tb["sin"], 2 * d_model), seq=seq,
                              silu_cols=2 * d_model)
            mix = _retention(proj, tb, batch=batch, seq=seq, d_model=d_model)
        last = i == depth - 1
        h, nxt = _tail(h, mix, p2, i, w_out, mlp_norm[i], w_up, w_down, ple_norm[i], w_gate,
                       ple_w_up[i].astype(BF16), final_norm, final_norm=last,
                       cast=() if last else layer_weights(i + 1))
        if not last:
            w_in, w_out, w_up, w_down, w_gate = nxt
    return h.reshape(batch, seq, d_model)
```

```python
import functools
import math

import jax
import jax.numpy as jnp
import numpy as np
from jax import lax
from jax.experimental import pallas as pl
from jax.experimental.pallas import tpu as pltpu

F32 = jnp.float32
BF16 = jnp.bfloat16

EPS = 1e-6
SB_HEADS = 16
RET_HEADS = 4
ROPE_BASE = 10000.0

VMEM_LIMIT_BYTES = 56 * 1024 * 1024
LANES = 128
LOG2E = math.log2(math.e)
MASK_BIG = 1e30
PZ_MAX = 126.0
SKIP_LOG2 = -160.0

PROJ_ROW_TILE = 1024
SB_PROJ_COL_TILE = 1536
RET_PROJ_COL_TILE = 1024
TAIL_ROW_TILE = 512
MLP_FF_TILE = 1024
SB_TILE = 256
RET_CHUNK = 256
RET_CHUNKS_PER_STEP = 16


def _params(*sem):
    return pltpu.CompilerParams(dimension_semantics=sem, vmem_limit_bytes=VMEM_LIMIT_BYTES)


def _rms_norm(x, g):
    ms = jnp.mean(x * x, axis=-1, keepdims=True)
    return x * lax.rsqrt(ms + EPS) * g


def _norm_proj_kernel(*refs, tn, scaled, rope_chunks, silu_from, n_cast):
    x_ref, g_ref, w_ref = refs[:3]
    n_in = 3 + int(scaled) + (2 if rope_chunks else 0)
    cast_in = refs[n_in:n_in + n_cast]
    o_ref = refs[n_in + n_cast]
    cast_out = refs[n_in + n_cast + 1:]
    xn = _rms_norm(x_ref[...], g_ref[...]).astype(BF16)
    if rope_chunks:
        cos = refs[n_in - 2][...]
        sin = refs[n_in - 1][...]
        hd = cos.shape[1]
    for c in range(w_ref.shape[1] // tn):
        cols = slice(c * tn, (c + 1) * tn)
        acc = jnp.dot(xn, w_ref[:, cols], preferred_element_type=F32)
        if scaled:
            acc = acc * refs[3][:, cols]
        if silu_from is not None and c >= silu_from:
            hx = 0.5 * acc
            o_ref[:, cols] = (hx + hx * jnp.tanh(hx)).astype(o_ref.dtype)
        elif c < rope_chunks:
            for h in range(tn // (2 * hd)):
                lo = c * tn + 2 * h * hd
                x1 = acc[:, 2 * h * hd:(2 * h + 1) * hd]
                x2 = acc[:, (2 * h + 1) * hd:(2 * h + 2) * hd]
                o_ref[:, lo:lo + hd] = (x1 * cos - x2 * sin).astype(o_ref.dtype)
                o_ref[:, lo + hd:lo + 2 * hd] = (x1 * sin + x2 * cos).astype(o_ref.dtype)
        else:
            o_ref[:, cols] = acc.astype(o_ref.dtype)

    for src, dst in zip(cast_in, cast_out):
        dst[...] = src[...].astype(dst.dtype)


def _cast_specs(cast, steps):
    ins = [pl.BlockSpec((None, w.shape[1] // steps, w.shape[2]), lambda i, k=k: (k, i, 0))
           for w, k in cast]
    outs = [pl.BlockSpec((w.shape[1] // steps, w.shape[2]), lambda i: (i, 0)) for w, _ in cast]
    shapes = [jax.ShapeDtypeStruct(w.shape[1:], BF16) for w, _ in cast]
    return ins, outs, shapes


def _norm_proj(h, g, w, *, tn, col_scale=None, rope=None, seq=None, silu_cols=None, cast=()):
    t, d = h.shape
    n = w.shape[1]
    tm = PROJ_ROW_TILE

    def whole(shape):
        return pl.BlockSpec(shape, lambda i: (0, 0), pipeline_mode=pl.Buffered(1))

    in_specs = [pl.BlockSpec((tm, d), lambda i: (i, 0)), whole((1, d)), whole((d, n))]
    args = [h, g.reshape(1, d), w]
    rope_chunks = 0
    if col_scale is not None:
        in_specs.append(whole((1, n)))
        args.append(col_scale.reshape(1, n))
    if rope is not None:
        cos, sin, rope_cols = rope
        rope_chunks = rope_cols // tn
        tiles_per_seq = seq // tm
        for tbl in (cos, sin):
            in_specs.append(pl.BlockSpec((tm, tbl.shape[1]), lambda i: (i % tiles_per_seq, 0)))
            args.append(tbl)
    cast_ins, cast_outs, cast_shapes = _cast_specs(cast, t // tm)
    outs = pl.pallas_call(
        functools.partial(_norm_proj_kernel, tn=tn, scaled=col_scale is not None,
                          rope_chunks=rope_chunks,
                          silu_from=None if silu_cols is None else (n - silu_cols) // tn,
                          n_cast=len(cast)),
        grid=(t // tm,),
        in_specs=in_specs + cast_ins,
        out_specs=[pl.BlockSpec((tm, n), lambda i: (i, 0))] + cast_outs,
        out_shape=[jax.ShapeDtypeStruct((t, n), BF16)] + cast_shapes,
        compiler_params=_params("parallel"),
        name="norm_proj",
    )(*args, *[w for w, _ in cast])
    return outs[0], outs[1:]


def _sb_kernel(q_ref, k_ref, v_ref, mask_ref, incl_ref, o_ref, pz_buf, sfx_buf, acc_ref,
               carry_ref, lst_q, lst_k, done_ref, *, tile, nq):
    half = LANES // 2
    lane = lax.broadcasted_iota(jnp.int32, (1, LANES), 1)
    head0 = lane < half

    @pl.when((pl.program_id(0) == 0) & (pl.program_id(1) == 0))
    def _():
        pz_buf[...] = jnp.zeros_like(pz_buf)
        sfx_buf[...] = jnp.zeros_like(sfx_buf)
        acc_ref[...] = jnp.zeros_like(acc_ref)
        carry_ref[...] = jnp.zeros_like(carry_ref)

    for i in range(nq + 1):
        done_ref[i] = 0

    def rows(idx):
        if isinstance(idx, int):
            return slice(idx * tile, (idx + 1) * tile)
        return pl.ds(pl.multiple_of(idx * tile, tile), tile)


    def stage_z(st, kj, slot, fresh, flag):
        suffix = sfx_buf[slot]
        if fresh is True:
            carry = suffix[:, 0:1]
            a = jnp.exp2(suffix + pz_buf[slot])
        else:
            carry = carry_ref[st] if fresh is False else jnp.where(fresh, 0.0, carry_ref[st])
            a = jnp.exp2(suffix + pz_buf[slot] + carry)
            carry = carry + suffix[:, 0:1]
        carry_ref[st] = carry
        if flag:
            done_ref[st] = (jnp.max(carry) < SKIP_LOG2).astype(jnp.int32)
        res = jnp.dot(a.astype(BF16), v_ref[rows(kj), :], preferred_element_type=F32)
        res = jnp.where(head0, res[:tile], res[tile:])
        if fresh is True:
            acc_ref[st] = res
        elif fresh is False:
            acc_ref[st] = res + acc_ref[st]
        else:
            acc_ref[st] = res + jnp.where(fresh, 0.0, acc_ref[st])

    def stage_y(slot):
        l = jnp.log(1.0 + jnp.exp2(pz_buf[slot])) * LOG2E
        sfx_buf[slot] = jnp.dot(l.astype(BF16), incl_ref[...], preferred_element_type=F32)

    def stage_x(qi, kj, slot, diag):
        q = q_ref[rows(qi), :]
        qzero = jnp.zeros_like(q)
        q2 = jnp.concatenate([jnp.where(head0, q, qzero), jnp.where(head0, qzero, q)], axis=0)
        pz_raw = lax.dot_general(q2, k_ref[rows(kj), :], (((1,), (1,)), ((), ())),
                                 preferred_element_type=F32)
        pz_buf[slot] = jnp.minimum(pz_raw, mask_ref[diag])

    first = [(qi, qi) for qi in range(nq)] + [(qi, qi - 1) for qi in range(1, nq)]
    for s in range(len(first) + 2):
        if s >= 2:
            qi, kj = first[s - 2]
            stage_z(qi, kj, s % 2, qi == kj, qi != kj)
        if 1 <= s <= len(first):
            stage_y((s - 1) % 2)
        if s < len(first):
            qi, kj = first[s]
            stage_x(qi, kj, s % 2, int(qi == kj))

    def entry(e):
        st = lst_q[e]
        return st, jnp.minimum(st, nq - 1), lst_k[e]

    def sub_step(e, slot):
        st, _, kj = entry(e)
        stage_z(st, kj, slot, st == nq, True)
        stage_y(1 - slot)
        _, qi, kj = entry(e + 2)
        stage_x(qi, kj, slot, 0)

    def put(i, st, kj):
        lst_q[i] = st
        lst_k[i] = kj

    def offset_body(state):
        d, _ = state
        put(0, nq, 0)
        put(1, nq, 0)
        n = 0
        for off in range(2):
            def add(qi, n, off=off):
                put(2 + n, qi, qi - d - off)
                return n + (done_ref[qi] == 0).astype(jnp.int32)

            n = lax.fori_loop(d + off, nq, add, n)
        for t in range(4):
            put(2 + n + t, nq, 0)

        def body(i, c):
            sub_step(2 * i, 0)
            sub_step(2 * i + 1, 1)
            return c

        lax.fori_loop(0, jnp.where(n > 0, (n + 3) // 2, 0), body, 0)
        return d + 2, n

    lax.while_loop(lambda st: (st[0] < nq) & (st[1] > 0), offset_body, (2, 1))

    for qi in range(nq):
        o_ref[qi * tile:(qi + 1) * tile, :] = acc_ref[qi].astype(o_ref.dtype)


def _sb_constants(tile):
    row = np.arange(tile)[:, None]
    col = np.arange(tile)[None, :]
    diag = np.where(col >= row, -MASK_BIG, PZ_MAX).astype(np.float32)
    diag = np.concatenate([diag, diag], axis=0)
    mask = np.stack([np.full_like(diag, PZ_MAX), diag])
    incl = np.where(row >= col, -1.0, 0.0).astype(np.float32)
    return jnp.asarray(mask), jnp.asarray(incl, dtype=BF16)


def _sb_attention(qkv, *, batch, seq, d_model):
    tile = SB_TILE
    nq = seq // tile
    pairs = SB_HEADS // 2
    qkv3 = qkv.reshape(batch, seq, 3 * d_model)
    mask, incl = _sb_constants(tile)
    kern = functools.partial(_sb_kernel, tile=tile, nq=nq)
    out = pl.pallas_call(
        kern,
        grid=(batch, pairs),
        in_specs=[
            pl.BlockSpec((None, seq, LANES), lambda b, p: (b, 0, p)),
            pl.BlockSpec((None, seq, LANES), lambda b, p: (b, 0, pairs + p)),
            pl.BlockSpec((None, seq, LANES), lambda b, p: (b, 0, 2 * pairs + p)),
            pl.BlockSpec((2, 2 * tile, tile), lambda b, p: (0, 0, 0), pipeline_mode=pl.Buffered(1)),
            pl.BlockSpec((tile, tile), lambda b, p: (0, 0), pipeline_mode=pl.Buffered(1)),
        ],
        out_specs=pl.BlockSpec((None, seq, LANES), lambda b, p: (b, 0, p)),
        out_shape=jax.ShapeDtypeStruct((batch, seq, d_model), BF16),
        scratch_shapes=[
            pltpu.VMEM((2, 2 * tile, tile), F32),
            pltpu.VMEM((2, 2 * tile, tile), F32),
            pltpu.VMEM((nq + 1, tile, LANES), F32),
            pltpu.VMEM((nq + 1, 2 * tile, 1), F32),
            pltpu.SMEM((2 * nq + 8,), jnp.int32),
            pltpu.SMEM((2 * nq + 8,), jnp.int32),
            pltpu.SMEM((nq + 1,), jnp.int32),
        ],
        compiler_params=_params("arbitrary", "arbitrary"),
        name="sb_attention",
    )(qkv3, qkv3, qkv3, mask, incl)
    return out.reshape(batch * seq, d_model)


def _ret_kernel(q_ref, k_ref, v_ref, g_ref, intra_ref, cross_ref, kv_ref, cd_ref, o_ref,
                state_ref, *, chunk, n_sub):
    @pl.when(pl.program_id(2) == 0)
    def _():
        state_ref[...] = jnp.zeros_like(state_ref)

    cross_d = cross_ref[...]
    kv_d = kv_ref[...]

    for c in range(n_sub):
        rs = slice(c * chunk, (c + 1) * chunk)
        qr = q_ref[rs, :]
        kr = k_ref[rs, :]
        v = v_ref[rs, :]

        scores = lax.dot_general(qr, kr, (((1,), (1,)), ((), ())),
                                 preferred_element_type=F32) * intra_ref[...]
        inner = jnp.dot(scores.astype(BF16), v, preferred_element_type=F32)

        state = state_ref[...]
        qc = (qr.astype(F32) * cross_d).astype(BF16)
        cross = jnp.dot(qc, state.astype(BF16), preferred_element_type=F32)

        kd = (kr.astype(F32) * kv_d).astype(BF16)
        state_ref[...] = state * cd_ref[...] + lax.dot_general(
            kd, v, (((0,), (0,)), ((), ())), preferred_element_type=F32)

        y = inner + cross
        y = y * lax.rsqrt(jnp.mean(y * y, axis=-1, keepdims=True) + EPS)
        o_ref[rs, :] = (g_ref[rs, :].astype(F32) * y).astype(o_ref.dtype)


def _retention_tables(seq, key_dim, val_dim, chunk):
    k_scale = key_dim ** -0.5
    inv_freq = ROPE_BASE ** (-np.arange(0, key_dim, 2, dtype=np.float64) / key_dim)
    ang = np.arange(seq, dtype=np.float64)[:, None] * inv_freq[None, :]
    log_gamma = np.log1p(-np.exp2(-5.0 - np.arange(RET_HEADS, dtype=np.float64)))
    idx = np.arange(chunk, dtype=np.float64)
    diff = idx[:, None] - idx[None, :]
    intra = np.where(diff >= 0, np.exp(log_gamma[:, None, None] * np.maximum(diff, 0.0)), 0.0)
    cross = np.exp(log_gamma[:, None] * (idx + 1.0))
    kv = np.exp(log_gamma[:, None] * (chunk - 1.0 - idx))
    cd = np.exp(log_gamma * chunk)
    tables = dict(
        cos=np.cos(ang), sin=np.sin(ang), intra=intra * k_scale,
        cross=np.broadcast_to(cross[:, :, None], (RET_HEADS, chunk, key_dim)),
        kv=np.broadcast_to((kv * k_scale)[:, :, None], (RET_HEADS, chunk, key_dim)),
        cd=np.broadcast_to(cd[:, None, None], (RET_HEADS, 1, val_dim)),
    )
    return {name: jnp.asarray(t, dtype=F32) for name, t in tables.items()}


def _retention(proj, tb, *, batch, seq, d_model):
    chunk = RET_CHUNK
    key_dim = d_model // RET_HEADS
    val_dim = 2 * d_model // RET_HEADS
    proj3 = proj.reshape(batch, seq, 6 * d_model)
    nh = RET_HEADS
    n_sub = RET_CHUNKS_PER_STEP
    blk = chunk * n_sub
    kern = functools.partial(_ret_kernel, chunk=chunk, n_sub=n_sub)
    out = pl.pallas_call(
        kern,
        grid=(batch, nh, seq // blk),
        in_specs=[
            pl.BlockSpec((None, blk, key_dim), lambda b, h, c: (b, c, h)),
            pl.BlockSpec((None, blk, key_dim), lambda b, h, c: (b, c, nh + h)),
            pl.BlockSpec((None, blk, val_dim), lambda b, h, c: (b, c, nh + h)),
            pl.BlockSpec((None, blk, val_dim), lambda b, h, c: (b, c, 2 * nh + h)),
            pl.BlockSpec((None, chunk, chunk), lambda b, h, c: (h, 0, 0)),
            pl.BlockSpec((None, chunk, key_dim), lambda b, h, c: (h, 0, 0)),
            pl.BlockSpec((None, chunk, key_dim), lambda b, h, c: (h, 0, 0)),
            pl.BlockSpec((None, 1, val_dim), lambda b, h, c: (h, 0, 0)),
        ],
        out_specs=pl.BlockSpec((None, blk, val_dim), lambda b, h, c: (b, c, h)),
        out_shape=jax.ShapeDtypeStruct((batch, seq, nh * val_dim), BF16),
        scratch_shapes=[pltpu.VMEM((key_dim, val_dim), F32)],
        compiler_params=_params("parallel", "parallel", "arbitrary"),
        name="retention",
    )(proj3, proj3, proj3, proj3, tb["intra"], tb["cross"], tb["kv"], tb["cd"])
    return out.reshape(batch * seq, nh * val_dim)


def _tail_kernel(*refs, final_norm, tf, n_cast):
    (h_ref, mix_ref, p_ref, wo_ref, gm_ref, wu_ref, wd_ref, gp_ref, wg_ref, wp_ref,
     fg_ref) = refs[:11]
    cast_in = refs[11:11 + n_cast]
    o_ref = refs[11 + n_cast]
    cast_out = refs[12 + n_cast:]

    h1 = h_ref[...] + jnp.dot(mix_ref[...], wo_ref[...], preferred_element_type=F32)
    xn = _rms_norm(h1, gm_ref[...]).astype(BF16)
    h2 = h1
    for c in range(wu_ref.shape[1] // tf):
        a = jnp.maximum(jnp.dot(xn, wu_ref[:, c * tf:(c + 1) * tf], preferred_element_type=F32), 0.0)
        h2 = h2 + jnp.dot((a * a).astype(BF16), wd_ref[c * tf:(c + 1) * tf, :],
                          preferred_element_type=F32)
    xn = _rms_norm(h2, gp_ref[...]).astype(BF16)
    gate = jax.nn.sigmoid(jnp.dot(xn, wg_ref[...], preferred_element_type=F32))
    up = jnp.dot(p_ref[...].astype(BF16), wp_ref[...], preferred_element_type=F32)
    out = h2 + up * gate
    if final_norm:
        out = _rms_norm(out, fg_ref[...])
    o_ref[...] = out

    for src, dst in zip(cast_in, cast_out):
        dst[...] = src[...].astype(dst.dtype)


def _tail(h, mix, p, layer, w_out, g_mlp, w_up, w_down, g_ple, w_gate, w_ple, final_g, *,
          final_norm, cast=()):
    t, d = h.shape
    dm = mix.shape[1]
    dp = p.shape[2]
    f = w_up.shape[1]
    tm = TAIL_ROW_TILE
    steps = t // tm

    def whole(shape):
        return pl.BlockSpec(shape, lambda i: (0, 0), pipeline_mode=pl.Buffered(1))

    cast_specs, cast_out_specs, cast_shapes = _cast_specs(cast, steps)
    outs = pl.pallas_call(
        functools.partial(_tail_kernel, final_norm=final_norm, tf=MLP_FF_TILE, n_cast=len(cast)),
        grid=(steps,),
        in_specs=[
            pl.BlockSpec((tm, d), lambda i: (i, 0)),
            pl.BlockSpec((tm, dm), lambda i: (i, 0)),
            pl.BlockSpec((None, tm, dp), lambda i: (layer, i, 0)),
            whole((dm, d)),
            whole((1, d)),
            whole((d, f)),
            whole((f, d)),
            whole((1, d)),
            whole((d, d)),
            whole((dp, d)),
            whole((1, d)),
        ] + cast_specs,
        out_specs=[pl.BlockSpec((tm, d), lambda i: (i, 0))] + cast_out_specs,
        out_shape=[jax.ShapeDtypeStruct((t, d), F32)] + cast_shapes,
        compiler_params=_params("parallel"),
        name="tail",
    )(h, mix, p, w_out, g_mlp.reshape(1, d), w_up, w_down, g_ple.reshape(1, d), w_gate, w_ple,
      final_g.reshape(1, d), *[w for w, _ in cast])
    return outs[0], outs[1:]


def kernel(x, p, mix_norm, sb_w_in, sb_w_out, ret_w_in, ret_w_out, mlp_norm, mlp_w_up,
           mlp_w_down, ple_norm, ple_w_gate, ple_w_up, final_norm):
    batch, seq, d_model = x.shape
    depth = p.shape[0]
    t = batch * seq
    h = x.reshape(t, d_model)
    p2 = p.reshape(depth, t, p.shape[-1])

    def layer_weights(i):
        mixer = (sb_w_in, sb_w_out) if i % 2 == 0 else (ret_w_in, ret_w_out)
        return [(mixer[0], i // 2), (mixer[1], i // 2), (mlp_w_up, i), (mlp_w_down, i),
                (ple_w_gate, i)]

    first_w, first_k = layer_weights(0)[0]
    w_in = first_w[first_k].astype(BF16)
    for i in range(depth):
        own = layer_weights(0)[1:] if i == 0 else ()
        if i % 2 == 0:
            q_scale = LOG2E * (d_model // SB_HEADS) ** -0.5
            col_scale = jnp.concatenate([jnp.full((d_model,), q_scale, F32), jnp.ones((2 * d_model,), F32)])
            qkv, cast_own = _norm_proj(h, mix_norm[i], w_in, tn=SB_PROJ_COL_TILE,
                                       col_scale=col_scale, cast=own)
            mix = _sb_attention(qkv, batch=batch, seq=seq, d_model=d_model)
        else:
            tb = _retention_tables(seq, d_model // RET_HEADS, 2 * d_model // RET_HEADS, RET_CHUNK)
            proj, cast_own = _norm_proj(h, mix_norm[i], w_in, tn=RET_PROJ_COL_TILE,
                                        rope=(tb["cos"], tb["sin"], 2 * d_model), seq=seq,
                                        silu_cols=2 * d_model, cast=own)
            mix = _retention(proj, tb, batch=batch, seq=seq, d_model=d_model)
        if own:
            w_out, w_up, w_down, w_gate = cast_own
        last = i == depth - 1
        h, nxt = _tail(h, mix, p2, i, w_out, mlp_norm[i], w_up, w_down, ple_norm[i], w_gate,
                       ple_w_up[i].astype(BF16), final_norm, final_norm=last,
                       cast=() if last else layer_weights(i + 1))
        if not last:
            w_in, w_out, w_up, w_down, w_gate = nxt
    return h.reshape(batch, seq, d_model)
```

```python
import functools
import math

import jax
import jax.numpy as jnp
import numpy as np
from jax import lax
from jax.experimental import pallas as pl
from jax.experimental.pallas import tpu as pltpu

F32 = jnp.float32
BF16 = jnp.bfloat16

EPS = 1e-6
SB_HEADS = 16
RET_HEADS = 4
ROPE_BASE = 10000.0

VMEM_LIMIT_BYTES = 56 * 1024 * 1024
LANES = 128
LOG2E = math.log2(math.e)
MASK_BIG = 1e30
PZ_MAX = 126.0
SKIP_LOG2 = -160.0

PROJ_ROW_TILE = 1024
SB_PROJ_COL_TILE = 1536
RET_PROJ_COL_TILE = 1024
TAIL_ROW_TILE = 512
MLP_FF_TILE = 1024
SB_TILE = 256
RET_CHUNK = 256
RET_CHUNKS_PER_STEP = 16


def _params(*sem):
    return pltpu.CompilerParams(dimension_semantics=sem, vmem_limit_bytes=VMEM_LIMIT_BYTES)


def _rms_norm(x, g):
    ms = jnp.mean(x * x, axis=-1, keepdims=True)
    return x * lax.rsqrt(ms + EPS) * g


def _norm_proj_kernel(*refs, tn, scaled, rope_chunks, silu_from, n_cast):
    x_ref, g_ref, w_ref = refs[:3]
    n_in = 3 + int(scaled) + (2 if rope_chunks else 0)
    cast_in = refs[n_in:n_in + n_cast]
    o_ref = refs[n_in + n_cast]
    cast_out = refs[n_in + n_cast + 1:]
    xn = _rms_norm(x_ref[...], g_ref[...]).astype(BF16)
    if rope_chunks:
        cos = refs[n_in - 2][...]
        sin = refs[n_in - 1][...]
        hd = cos.shape[1]
    for c in range(w_ref.shape[1] // tn):
        cols = slice(c * tn, (c + 1) * tn)
        acc = jnp.dot(xn, w_ref[:, cols], preferred_element_type=F32)
        if scaled:
            acc = acc * refs[3][:, cols]
        if silu_from is not None and c >= silu_from:
            hx = 0.5 * acc
            o_ref[:, cols] = (hx + hx * jnp.tanh(hx)).astype(o_ref.dtype)
        elif c < rope_chunks:
            for h in range(tn // (2 * hd)):
                lo = c * tn + 2 * h * hd
                x1 = acc[:, 2 * h * hd:(2 * h + 1) * hd]
                x2 = acc[:, (2 * h + 1) * hd:(2 * h + 2) * hd]
                o_ref[:, lo:lo + hd] = (x1 * cos - x2 * sin).astype(o_ref.dtype)
                o_ref[:, lo + hd:lo + 2 * hd] = (x1 * sin + x2 * cos).astype(o_ref.dtype)
        else:
            o_ref[:, cols] = acc.astype(o_ref.dtype)

    for src, dst in zip(cast_in, cast_out):
        dst[...] = src[...].astype(dst.dtype)


def _cast_specs(cast, steps):
    ins = [pl.BlockSpec((None, w.shape[1] // steps, w.shape[2]), lambda i, k=k: (k, i, 0))
           for w, k in cast]
    outs = [pl.BlockSpec((w.shape[1] // steps, w.shape[2]), lambda i: (i, 0)) for w, _ in cast]
    shapes = [jax.ShapeDtypeStruct(w.shape[1:], BF16) for w, _ in cast]
    return ins, outs, shapes


def _norm_proj(h, g, w, *, tn, col_scale=None, rope=None, seq=None, silu_cols=None, cast=()):
    t, d = h.shape
    n = w.shape[1]
    tm = PROJ_ROW_TILE

    def whole(shape):
        return pl.BlockSpec(shape, lambda i: (0, 0), pipeline_mode=pl.Buffered(1))

    in_specs = [pl.BlockSpec((tm, d), lambda i: (i, 0)), whole((1, d)), whole((d, n))]
    args = [h, g.reshape(1, d), w]
    rope_chunks = 0
    if col_scale is not None:
        in_specs.append(whole((1, n)))
        args.append(col_scale.reshape(1, n))
    if rope is not None:
        cos, sin, rope_cols = rope
        rope_chunks = rope_cols // tn
        tiles_per_seq = seq // tm
        for tbl in (cos, sin):
            in_specs.append(pl.BlockSpec((tm, tbl.shape[1]), lambda i: (i % tiles_per_seq, 0)))
            args.append(tbl)
    cast_ins, cast_outs, cast_shapes = _cast_specs(cast, t // tm)
    outs = pl.pallas_call(
        functools.partial(_norm_proj_kernel, tn=tn, scaled=col_scale is not None,
                          rope_chunks=rope_chunks,
                          silu_from=None if silu_cols is None else (n - silu_cols) // tn,
                          n_cast=len(cast)),
        grid=(t // tm,),
        in_specs=in_specs + cast_ins,
        out_specs=[pl.BlockSpec((tm, n), lambda i: (i, 0))] + cast_outs,
        out_shape=[jax.ShapeDtypeStruct((t, n), BF16)] + cast_shapes,
        compiler_params=_params("parallel"),
        name="norm_proj",
    )(*args, *[w for w, _ in cast])
    return outs[0], outs[1:]


def _sb_kernel(q_ref, k_ref, v_ref, mask_ref, incl_ref, o_ref, pz_buf, sfx_buf, acc_ref,
               carry_ref, lst_q, lst_k, done_ref, *, tile, nq):
    half = LANES // 2
    lane = lax.broadcasted_iota(jnp.int32, (1, LANES), 1)
    head0 = lane < half

    @pl.when((pl.program_id(0) == 0) & (pl.program_id(1) == 0))
    def _():
        pz_buf[...] = jnp.zeros_like(pz_buf)
        sfx_buf[...] = jnp.zeros_like(sfx_buf)
        acc_ref[...] = jnp.zeros_like(acc_ref)
        carry_ref[...] = jnp.zeros_like(carry_ref)

    for i in range(nq + 1):
        done_ref[i] = 0

    def rows(idx):
        if isinstance(idx, int):
            return slice(idx * tile, (idx + 1) * tile)
        return pl.ds(pl.multiple_of(idx * tile, tile), tile)


    def stage_z(st, kj, slot, fresh, flag):
        suffix = sfx_buf[slot]
        if fresh is True:
            carry = suffix[:, 0:1]
            a = jnp.exp2(suffix + pz_buf[slot])
        else:
            carry = carry_ref[st] if fresh is False else jnp.where(fresh, 0.0, carry_ref[st])
            a = jnp.exp2(suffix + pz_buf[slot] + carry)
            carry = carry + suffix[:, 0:1]
        carry_ref[st] = carry
        if flag:
            done_ref[st] = (jnp.max(carry) < SKIP_LOG2).astype(jnp.int32)
        res = jnp.dot(a.astype(BF16), v_ref[rows(kj), :], preferred_element_type=F32)
        res = jnp.where(head0, res[:tile], res[tile:])
        if fresh is True:
            acc_ref[st] = res
        elif fresh is False:
            acc_ref[st] = res + acc_ref[st]
        else:
            acc_ref[st] = res + jnp.where(fresh, 0.0, acc_ref[st])

    def stage_y(slot):
        l = jnp.log(1.0 + jnp.exp2(pz_buf[slot])) * LOG2E
        sfx_buf[slot] = jnp.dot(l.astype(BF16), incl_ref[...], preferred_element_type=F32)

    def stage_x(qi, kj, slot, diag):
        q = q_ref[rows(qi), :]
        qzero = jnp.zeros_like(q)
        q2 = jnp.concatenate([jnp.where(head0, q, qzero), jnp.where(head0, qzero, q)], axis=0)
        pz_raw = lax.dot_general(q2, k_ref[rows(kj), :], (((1,), (1,)), ((), ())),
                                 preferred_element_type=F32)
        pz_buf[slot] = jnp.minimum(pz_raw, mask_ref[diag])

    first = [(qi, qi) for qi in range(nq)] + [(qi, qi - 1) for qi in range(1, nq)]
    for s in range(len(first) + 2):
        if s >= 2:
            qi, kj = first[s - 2]
            stage_z(qi, kj, s % 2, qi == kj, qi != kj)
        if 1 <= s <= len(first):
            stage_y((s - 1) % 2)
        if s < len(first):
            qi, kj = first[s]
            stage_x(qi, kj, s % 2, int(qi == kj))

    def entry(e):
        st = lst_q[e]
        return st, jnp.minimum(st, nq - 1), lst_k[e]

    def sub_step(e, slot):
        st, _, kj = entry(e)
        stage_z(st, kj, slot, st == nq, True)
        stage_y(1 - slot)
        _, qi, kj = entry(e + 2)
        stage_x(qi, kj, slot, 0)

    def put(i, st, kj):
        lst_q[i] = st
        lst_k[i] = kj

    def offset_body(state):
        d, _ = state
        put(0, nq, 0)
        put(1, nq, 0)
        n = 0
        for off in range(2):
            def add(qi, n, off=off):
                put(2 + n, qi, qi - d - off)
                return n + (done_ref[qi] == 0).astype(jnp.int32)

            n = lax.fori_loop(d + off, nq, add, n)
        for t in range(4):
            put(2 + n + t, nq, 0)

        def body(i, c):
            sub_step(2 * i, 0)
            sub_step(2 * i + 1, 1)
            return c

        lax.fori_loop(0, jnp.where(n > 0, (n + 3) // 2, 0), body, 0)
        return d + 2, n

    pending = sum((done_ref[qi] == 0).astype(jnp.int32) for qi in range(2, nq))
    lax.while_loop(lambda st: (st[0] < nq) & (st[1] > 0), offset_body, (2, pending))

    for qi in range(nq):
        o_ref[qi * tile:(qi + 1) * tile, :] = acc_ref[qi].astype(o_ref.dtype)


def _sb_constants(tile):
    row = np.arange(tile)[:, None]
    col = np.arange(tile)[None, :]
    diag = np.where(col >= row, -MASK_BIG, PZ_MAX).astype(np.float32)
    diag = np.concatenate([diag, diag], axis=0)
    mask = np.stack([np.full_like(diag, PZ_MAX), diag])
    incl = np.where(row >= col, -1.0, 0.0).astype(np.float32)
    return jnp.asarray(mask), jnp.asarray(incl, dtype=BF16)


def _sb_attention(qkv, *, batch, seq, d_model):
    tile = SB_TILE
    nq = seq // tile
    pairs = SB_HEADS // 2
    qkv3 = qkv.reshape(batch, seq, 3 * d_model)
    mask, incl = _sb_constants(tile)
    kern = functools.partial(_sb_kernel, tile=tile, nq=nq)
    out = pl.pallas_call(
        kern,
        grid=(batch, pairs),
        in_specs=[
            pl.BlockSpec((None, seq, LANES), lambda b, p: (b, 0, p)),
            pl.BlockSpec((None, seq, LANES), lambda b, p: (b, 0, pairs + p)),
            pl.BlockSpec((None, seq, LANES), lambda b, p: (b, 0, 2 * pairs + p)),
            pl.BlockSpec((2, 2 * tile, tile), lambda b, p: (0, 0, 0), pipeline_mode=pl.Buffered(1)),
            pl.BlockSpec((tile, tile), lambda b, p: (0, 0), pipeline_mode=pl.Buffered(1)),
        ],
        out_specs=pl.BlockSpec((None, seq, LANES), lambda b, p: (b, 0, p)),
        out_shape=jax.ShapeDtypeStruct((batch, seq, d_model), BF16),
        scratch_shapes=[
            pltpu.VMEM((2, 2 * tile, tile), F32),
            pltpu.VMEM((2, 2 * tile, tile), F32),
            pltpu.VMEM((nq + 1, tile, LANES), F32),
            pltpu.VMEM((nq + 1, 2 * tile, 1), F32),
            pltpu.SMEM((2 * nq + 8,), jnp.int32),
            pltpu.SMEM((2 * nq + 8,), jnp.int32),
            pltpu.SMEM((nq + 1,), jnp.int32),
        ],
        compiler_params=_params("arbitrary", "arbitrary"),
        name="sb_attention",
    )(qkv3, qkv3, qkv3, mask, incl)
    return out.reshape(batch * seq, d_model)


def _ret_kernel(q_ref, k_ref, v_ref, g_ref, intra_ref, cross_ref, kv_ref, cd_ref, o_ref,
                state_ref, *, chunk, n_sub):
    @pl.when(pl.program_id(2) == 0)
    def _():
        state_ref[...] = jnp.zeros_like(state_ref)

    cross_d = cross_ref[...]
    kv_d = kv_ref[...]

    for c in range(n_sub):
        rs = slice(c * chunk, (c + 1) * chunk)
        qr = q_ref[rs, :]
        kr = k_ref[rs, :]
        v = v_ref[rs, :]

        scores = lax.dot_general(qr, kr, (((1,), (1,)), ((), ())),
                                 preferred_element_type=F32) * intra_ref[...]
        inner = jnp.dot(scores.astype(BF16), v, preferred_element_type=F32)

        state = state_ref[...]
        qc = (qr.astype(F32) * cross_d).astype(BF16)
        cross = jnp.dot(qc, state.astype(BF16), preferred_element_type=F32)

        kd = (kr.astype(F32) * kv_d).astype(BF16)
        state_ref[...] = state * cd_ref[...] + lax.dot_general(
            kd, v, (((0,), (0,)), ((), ())), preferred_element_type=F32)

        y = inner + cross
        y = y * lax.rsqrt(jnp.mean(y * y, axis=-1, keepdims=True) + EPS)
        o_ref[rs, :] = (g_ref[rs, :].astype(F32) * y).astype(o_ref.dtype)


def _retention_tables(seq, key_dim, val_dim, chunk):
    k_scale = key_dim ** -0.5
    inv_freq = ROPE_BASE ** (-np.arange(0, key_dim, 2, dtype=np.float64) / key_dim)
    ang = np.arange(seq, dtype=np.float64)[:, None] * inv_freq[None, :]
    log_gamma = np.log1p(-np.exp2(-5.0 - np.arange(RET_HEADS, dtype=np.float64)))
    idx = np.arange(chunk, dtype=np.float64)
    diff = idx[:, None] - idx[None, :]
    intra = np.where(diff >= 0, np.exp(log_gamma[:, None, None] * np.maximum(diff, 0.0)), 0.0)
    cross = np.exp(log_gamma[:, None] * (idx + 1.0))
    kv = np.exp(log_gamma[:, None] * (chunk - 1.0 - idx))
    cd = np.exp(log_gamma * chunk)
    tables = dict(
        cos=np.cos(ang), sin=np.sin(ang), intra=intra * k_scale,
        cross=np.broadcast_to(cross[:, :, None], (RET_HEADS, chunk, key_dim)),
        kv=np.broadcast_to((kv * k_scale)[:, :, None], (RET_HEADS, chunk, key_dim)),
        cd=np.broadcast_to(cd[:, None, None], (RET_HEADS, 1, val_dim)),
    )
    return {name: jnp.asarray(t, dtype=F32) for name, t in tables.items()}


def _retention(proj, tb, *, batch, seq, d_model):
    chunk = RET_CHUNK
    key_dim = d_model // RET_HEADS
    val_dim = 2 * d_model // RET_HEADS
    proj3 = proj.reshape(batch, seq, 6 * d_model)
    nh = RET_HEADS
    n_sub = RET_CHUNKS_PER_STEP
    blk = chunk * n_sub
    kern = functools.partial(_ret_kernel, chunk=chunk, n_sub=n_sub)
    out = pl.pallas_call(
        kern,
        grid=(batch, nh, seq // blk),
        in_specs=[
            pl.BlockSpec((None, blk, key_dim), lambda b, h, c: (b, c, h)),
            pl.BlockSpec((None, blk, key_dim), lambda b, h, c: (b, c, nh + h)),
            pl.BlockSpec((None, blk, val_dim), lambda b, h, c: (b, c, nh + h)),
            pl.BlockSpec((None, blk, val_dim), lambda b, h, c: (b, c, 2 * nh + h)),
            pl.BlockSpec((None, chunk, chunk), lambda b, h, c: (h, 0, 0)),
            pl.BlockSpec((None, chunk, key_dim), lambda b, h, c: (h, 0, 0)),
            pl.BlockSpec((None, chunk, key_dim), lambda b, h, c: (h, 0, 0)),
            pl.BlockSpec((None, 1, val_dim), lambda b, h, c: (h, 0, 0)),
        ],
        out_specs=pl.BlockSpec((None, blk, val_dim), lambda b, h, c: (b, c, h)),
        out_shape=jax.ShapeDtypeStruct((batch, seq, nh * val_dim), BF16),
        scratch_shapes=[pltpu.VMEM((key_dim, val_dim), F32)],
        compiler_params=_params("parallel", "parallel", "arbitrary"),
        name="retention",
    )(proj3, proj3, proj3, proj3, tb["intra"], tb["cross"], tb["kv"], tb["cd"])
    return out.reshape(batch * seq, nh * val_dim)


def _tail_kernel(*refs, final_norm, tf, n_cast):
    (h_ref, mix_ref, p_ref, wo_ref, gm_ref, wu_ref, wd_ref, gp_ref, wg_ref, wp_ref,
     fg_ref) = refs[:11]
    cast_in = refs[11:11 + n_cast]
    o_ref = refs[11 + n_cast]
    cast_out = refs[12 + n_cast:]

    h1 = h_ref[...] + jnp.dot(mix_ref[...], wo_ref[...], preferred_element_type=F32)
    xn = _rms_norm(h1, gm_ref[...]).astype(BF16)
    h2 = h1
    for c in range(wu_ref.shape[1] // tf):
        a = jnp.maximum(jnp.dot(xn, wu_ref[:, c * tf:(c + 1) * tf], preferred_element_type=F32), 0.0)
        h2 = h2 + jnp.dot((a * a).astype(BF16), wd_ref[c * tf:(c + 1) * tf, :],
                          preferred_element_type=F32)
    xn = _rms_norm(h2, gp_ref[...]).astype(BF16)
    gate = jax.nn.sigmoid(jnp.dot(xn, wg_ref[...], preferred_element_type=F32))
    up = jnp.dot(p_ref[...].astype(BF16), wp_ref[...], preferred_element_type=F32)
    out = h2 + up * gate
    if final_norm:
        out = _rms_norm(out, fg_ref[...])
    o_ref[...] = out

    for src, dst in zip(cast_in, cast_out):
        dst[...] = src[...].astype(dst.dtype)


def _tail(h, mix, p, layer, w_out, g_mlp, w_up, w_down, g_ple, w_gate, w_ple, final_g, *,
          final_norm, cast=()):
    t, d = h.shape
    dm = mix.shape[1]
    dp = p.shape[2]
    f = w_up.shape[1]
    tm = TAIL_ROW_TILE
    steps = t // tm

    def whole(shape):
        return pl.BlockSpec(shape, lambda i: (0, 0), pipeline_mode=pl.Buffered(1))

    cast_specs, cast_out_specs, cast_shapes = _cast_specs(cast, steps)
    outs = pl.pallas_call(
        functools.partial(_tail_kernel, final_norm=final_norm, tf=MLP_FF_TILE, n_cast=len(cast)),
        grid=(steps,),
        in_specs=[
            pl.BlockSpec((tm, d), lambda i: (i, 0)),
            pl.BlockSpec((tm, dm), lambda i: (i, 0)),
            pl.BlockSpec((None, tm, dp), lambda i: (layer, i, 0)),
            whole((dm, d)),
            whole((1, d)),
            whole((d, f)),
            whole((f, d)),
            whole((1, d)),
            whole((d, d)),
            whole((dp, d)),
            whole((1, d)),
        ] + cast_specs,
        out_specs=[pl.BlockSpec((tm, d), lambda i: (i, 0))] + cast_out_specs,
        out_shape=[jax.ShapeDtypeStruct((t, d), F32)] + cast_shapes,
        compiler_params=_params("parallel"),
        name="tail",
    )(h, mix, p, w_out, g_mlp.reshape(1, d), w_up, w_down, g_ple.reshape(1, d), w_gate, w_ple,
      final_g.reshape(1, d), *[w for w, _ in cast])
    return outs[0], outs[1:]


def kernel(x, p, mix_norm, sb_w_in, sb_w_out, ret_w_in, ret_w_out, mlp_norm, mlp_w_up,
           mlp_w_down, ple_norm, ple_w_gate, ple_w_up, final_norm):
    batch, seq, d_model = x.shape
    depth = p.shape[0]
    t = batch * seq
    h = x.reshape(t, d_model)
    p2 = p.reshape(depth, t, p.shape[-1])

    def layer_weights(i):
        mixer = (sb_w_in, sb_w_out) if i % 2 == 0 else (ret_w_in, ret_w_out)
        return [(mixer[0], i // 2), (mixer[1], i // 2), (mlp_w_up, i), (mlp_w_down, i),
                (ple_w_gate, i)]

    first_w, first_k = layer_weights(0)[0]
    w_in = first_w[first_k].astype(BF16)
    for i in range(depth):
        own = layer_weights(0)[1:] if i == 0 else ()
        if i % 2 == 0:
            q_scale = LOG2E * (d_model // SB_HEADS) ** -0.5
            col_scale = jnp.concatenate([jnp.full((d_model,), q_scale, F32), jnp.ones((2 * d_model,), F32)])
            qkv, cast_own = _norm_proj(h, mix_norm[i], w_in, tn=SB_PROJ_COL_TILE,
                                       col_scale=col_scale, cast=own)
            mix = _sb_attention(qkv, batch=batch, seq=seq, d_model=d_model)
        else:
            tb = _retention_tables(seq, d_model // RET_HEADS, 2 * d_model // RET_HEADS, RET_CHUNK)
            proj, cast_own = _norm_proj(h, mix_norm[i], w_in, tn=RET_PROJ_COL_TILE,
                                        rope=(tb["cos"], tb["sin"], 2 * d_model), seq=seq,
                                        silu_cols=2 * d_model, cast=own)
            mix = _retention(proj, tb, batch=batch, seq=seq, d_model=d_model)
        if own:
            w_out, w_up, w_down, w_gate = cast_own
        last = i == depth - 1
        h, nxt = _tail(h, mix, p2, i, w_out, mlp_norm[i], w_up, w_down, ple_norm[i], w_gate,
                       ple_w_up[i].astype(BF16), final_norm, final_norm=last,
                       cast=() if last else layer_weights(i + 1))
        if not last:
            w_in, w_out, w_up, w_down, w_gate = nxt
    return h.reshape(batch, seq, d_model)
```

```python
import functools
import math

import jax
import jax.numpy as jnp
import numpy as np
from jax import lax
from jax.experimental import pallas as pl
from jax.experimental.pallas import tpu as pltpu

F32 = jnp.float32
BF16 = jnp.bfloat16

EPS = 1e-6
SB_HEADS = 16
RET_HEADS = 4
ROPE_BASE = 10000.0

VMEM_LIMIT_BYTES = 56 * 1024 * 1024
LANES = 128
LOG2E = math.log2(math.e)
MASK_BIG = 1e30
PZ_MAX = 126.0
SKIP_LOG2 = -160.0

PROJ_ROW_TILE = 1024
SB_PROJ_COL_TILE = 1536
RET_PROJ_COL_TILE = 1024
TAIL_ROW_TILE = 512
MLP_FF_TILE = 1024
SB_TILE = 256
RET_CHUNK = 256
RET_CHUNKS_PER_STEP = 8
RET_HEAD_GROUP = 2


def _params(*sem):
    return pltpu.CompilerParams(dimension_semantics=sem, vmem_limit_bytes=VMEM_LIMIT_BYTES)


def _rms_norm(x, g):
    ms = jnp.mean(x * x, axis=-1, keepdims=True)
    return x * lax.rsqrt(ms + EPS) * g


def _norm_proj_kernel(*refs, tn, scaled, rope_chunks, silu_from, n_cast):
    x_ref, g_ref, w_ref = refs[:3]
    n_in = 3 + int(scaled) + (2 if rope_chunks else 0)
    cast_in = refs[n_in:n_in + n_cast]
    o_ref = refs[n_in + n_cast]
    cast_out = refs[n_in + n_cast + 1:]
    xn = _rms_norm(x_ref[...], g_ref[...]).astype(BF16)
    if rope_chunks:
        cos = refs[n_in - 2][...]
        sin = refs[n_in - 1][...]
        hd = cos.shape[1]
    for c in range(w_ref.shape[1] // tn):
        cols = slice(c * tn, (c + 1) * tn)
        acc = jnp.dot(xn, w_ref[:, cols], preferred_element_type=F32)
        if scaled:
            acc = acc * refs[3][:, cols]
        if silu_from is not None and c >= silu_from:
            hx = 0.5 * acc
            o_ref[:, cols] = (hx + hx * jnp.tanh(hx)).astype(o_ref.dtype)
        elif c < rope_chunks:
            for h in range(tn // (2 * hd)):
                lo = c * tn + 2 * h * hd
                x1 = acc[:, 2 * h * hd:(2 * h + 1) * hd]
                x2 = acc[:, (2 * h + 1) * hd:(2 * h + 2) * hd]
                o_ref[:, lo:lo + hd] = (x1 * cos - x2 * sin).astype(o_ref.dtype)
                o_ref[:, lo + hd:lo + 2 * hd] = (x1 * sin + x2 * cos).astype(o_ref.dtype)
        else:
            o_ref[:, cols] = acc.astype(o_ref.dtype)

    for src, dst in zip(cast_in, cast_out):
        dst[...] = src[...].astype(dst.dtype)


def _cast_specs(cast, steps):
    ins = [pl.BlockSpec((None, w.shape[1] // steps, w.shape[2]), lambda i, k=k: (k, i, 0))
           for w, k in cast]
    outs = [pl.BlockSpec((w.shape[1] // steps, w.shape[2]), lambda i: (i, 0)) for w, _ in cast]
    shapes = [jax.ShapeDtypeStruct(w.shape[1:], BF16) for w, _ in cast]
    return ins, outs, shapes


def _norm_proj(h, g, w, *, tn, col_scale=None, rope=None, seq=None, silu_cols=None, cast=()):
    t, d = h.shape
    n = w.shape[1]
    tm = PROJ_ROW_TILE

    def whole(shape):
        return pl.BlockSpec(shape, lambda i: (0, 0), pipeline_mode=pl.Buffered(1))

    in_specs = [pl.BlockSpec((tm, d), lambda i: (i, 0)), whole((1, d)), whole((d, n))]
    args = [h, g.reshape(1, d), w]
    rope_chunks = 0
    if col_scale is not None:
        in_specs.append(whole((1, n)))
        args.append(col_scale.reshape(1, n))
    if rope is not None:
        cos, sin, rope_cols = rope
        rope_chunks = rope_cols // tn
        tiles_per_seq = seq // tm
        for tbl in (cos, sin):
            in_specs.append(pl.BlockSpec((tm, tbl.shape[1]), lambda i: (i % tiles_per_seq, 0)))
            args.append(tbl)
    cast_ins, cast_outs, cast_shapes = _cast_specs(cast, t // tm)
    outs = pl.pallas_call(
        functools.partial(_norm_proj_kernel, tn=tn, scaled=col_scale is not None,
                          rope_chunks=rope_chunks,
                          silu_from=None if silu_cols is None else (n - silu_cols) // tn,
                          n_cast=len(cast)),
        grid=(t // tm,),
        in_specs=in_specs + cast_ins,
        out_specs=[pl.BlockSpec((tm, n), lambda i: (i, 0))] + cast_outs,
        out_shape=[jax.ShapeDtypeStruct((t, n), BF16)] + cast_shapes,
        compiler_params=_params("parallel"),
        name="norm_proj",
    )(*args, *[w for w, _ in cast])
    return outs[0], outs[1:]


def _sb_kernel(q_ref, k_ref, v_ref, mask_ref, incl_ref, o_ref, pz_buf, sfx_buf, acc_ref,
               carry_ref, lst_q, lst_k, done_ref, *, tile, nq):
    half = LANES // 2
    lane = lax.broadcasted_iota(jnp.int32, (1, LANES), 1)
    head0 = lane < half

    @pl.when((pl.program_id(0) == 0) & (pl.program_id(1) == 0))
    def _():
        pz_buf[...] = jnp.zeros_like(pz_buf)
        sfx_buf[...] = jnp.zeros_like(sfx_buf)
        acc_ref[...] = jnp.zeros_like(acc_ref)
        carry_ref[...] = jnp.zeros_like(carry_ref)

    for i in range(nq + 1):
        done_ref[i] = 0

    def rows(idx):
        if isinstance(idx, int):
            return slice(idx * tile, (idx + 1) * tile)
        return pl.ds(pl.multiple_of(idx * tile, tile), tile)


    def stage_z(st, kj, slot, fresh, flag):
        suffix = sfx_buf[slot]
        if fresh is True:
            carry = suffix[:, 0:1]
            a = jnp.exp2(suffix + pz_buf[slot])
        else:
            carry = carry_ref[st] if fresh is False else jnp.where(fresh, 0.0, carry_ref[st])
            a = jnp.exp2(suffix + pz_buf[slot] + carry)
            carry = carry + suffix[:, 0:1]
        carry_ref[st] = carry
        if flag:
            done_ref[st] = (jnp.max(carry) < SKIP_LOG2).astype(jnp.int32)
        res = jnp.dot(a.astype(BF16), v_ref[rows(kj), :], preferred_element_type=F32)
        res = jnp.where(head0, res[:tile], res[tile:])
        if fresh is True:
            acc_ref[st] = res
        elif fresh is False:
            acc_ref[st] = res + acc_ref[st]
        else:
            acc_ref[st] = res + jnp.where(fresh, 0.0, acc_ref[st])

    def stage_y(slot):
        l = jnp.log(1.0 + jnp.exp2(pz_buf[slot])) * LOG2E
        sfx_buf[slot] = jnp.dot(l.astype(BF16), incl_ref[...], preferred_element_type=F32)

    def stage_x(qi, kj, slot, diag):
        q = q_ref[rows(qi), :]
        qzero = jnp.zeros_like(q)
        q2 = jnp.concatenate([jnp.where(head0, q, qzero), jnp.where(head0, qzero, q)], axis=0)
        pz_raw = lax.dot_general(q2, k_ref[rows(kj), :], (((1,), (1,)), ((), ())),
                                 preferred_element_type=F32)
        pz_buf[slot] = jnp.minimum(pz_raw, mask_ref[diag])

    first = [(qi, qi) for qi in range(nq)] + [(qi, qi - 1) for qi in range(1, nq)]
    for s in range(len(first) + 2):
        if s >= 2:
            qi, kj = first[s - 2]
            stage_z(qi, kj, s % 2, qi == kj, qi != kj)
        if 1 <= s <= len(first):
            stage_y((s - 1) % 2)
        if s < len(first):
            qi, kj = first[s]
            stage_x(qi, kj, s % 2, int(qi == kj))

    def entry(e):
        st = lst_q[e]
        return st, jnp.minimum(st, nq - 1), lst_k[e]

    def sub_step(e, slot):
        st, _, kj = entry(e)
        stage_z(st, kj, slot, st == nq, True)
        stage_y(1 - slot)
        _, qi, kj = entry(e + 2)
        stage_x(qi, kj, slot, 0)

    def put(i, st, kj):
        lst_q[i] = st
        lst_k[i] = kj

    def offset_body(state):
        d, _ = state
        put(0, nq, 0)
        put(1, nq, 0)
        n = 0
        for off in range(2):
            def add(qi, n, off=off):
                put(2 + n, qi, qi - d - off)
                return n + (done_ref[qi] == 0).astype(jnp.int32)

            n = lax.fori_loop(d + off, nq, add, n)
        for t in range(4):
            put(2 + n + t, nq, 0)

        def body(i, c):
            sub_step(2 * i, 0)
            sub_step(2 * i + 1, 1)
            return c

        lax.fori_loop(0, jnp.where(n > 0, (n + 3) // 2, 0), body, 0)
        return d + 2, n

    pending = sum((done_ref[qi] == 0).astype(jnp.int32) for qi in range(2, nq))
    lax.while_loop(lambda st: (st[0] < nq) & (st[1] > 0), offset_body, (2, pending))

    for qi in range(nq):
        o_ref[qi * tile:(qi + 1) * tile, :] = acc_ref[qi].astype(o_ref.dtype)


def _sb_constants(tile):
    row = np.arange(tile)[:, None]
    col = np.arange(tile)[None, :]
    diag = np.where(col >= row, -MASK_BIG, PZ_MAX).astype(np.float32)
    diag = np.concatenate([diag, diag], axis=0)
    mask = np.stack([np.full_like(diag, PZ_MAX), diag])
    incl = np.where(row >= col, -1.0, 0.0).astype(np.float32)
    return jnp.asarray(mask), jnp.asarray(incl, dtype=BF16)


def _sb_attention(qkv, *, batch, seq, d_model):
    tile = SB_TILE
    nq = seq // tile
    pairs = SB_HEADS // 2
    qkv3 = qkv.reshape(batch, seq, 3 * d_model)
    mask, incl = _sb_constants(tile)
    kern = functools.partial(_sb_kernel, tile=tile, nq=nq)
    out = pl.pallas_call(
        kern,
        grid=(batch, pairs),
        in_specs=[
            pl.BlockSpec((None, seq, LANES), lambda b, p: (b, 0, p)),
            pl.BlockSpec((None, seq, LANES), lambda b, p: (b, 0, pairs + p)),
            pl.BlockSpec((None, seq, LANES), lambda b, p: (b, 0, 2 * pairs + p)),
            pl.BlockSpec((2, 2 * tile, tile), lambda b, p: (0, 0, 0), pipeline_mode=pl.Buffered(1)),
            pl.BlockSpec((tile, tile), lambda b, p: (0, 0), pipeline_mode=pl.Buffered(1)),
        ],
        out_specs=pl.BlockSpec((None, seq, LANES), lambda b, p: (b, 0, p)),
        out_shape=jax.ShapeDtypeStruct((batch, seq, d_model), BF16),
        scratch_shapes=[
            pltpu.VMEM((2, 2 * tile, tile), F32),
            pltpu.VMEM((2, 2 * tile, tile), F32),
            pltpu.VMEM((nq + 1, tile, LANES), F32),
            pltpu.VMEM((nq + 1, 2 * tile, 1), F32),
            pltpu.SMEM((2 * nq + 8,), jnp.int32),
            pltpu.SMEM((2 * nq + 8,), jnp.int32),
            pltpu.SMEM((nq + 1,), jnp.int32),
        ],
        compiler_params=_params("arbitrary", "arbitrary"),
        name="sb_attention",
    )(qkv3, qkv3, qkv3, mask, incl)
    return out.reshape(batch * seq, d_model)


def _ret_kernel(q_ref, k_ref, v_ref, g_ref, intra_ref, cross_ref, kv_ref, cd_ref, o_ref,
                state_ref, *, chunk, n_sub, heads):
    @pl.when(pl.program_id(2) == 0)
    def _():
        state_ref[...] = jnp.zeros_like(state_ref)

    kd_w = q_ref.shape[1] // heads
    vd_w = v_ref.shape[1] // heads

    for c in range(n_sub):
        rs = slice(c * chunk, (c + 1) * chunk)
        for hh in range(heads):
            kc = slice(hh * kd_w, (hh + 1) * kd_w)
            vc = slice(hh * vd_w, (hh + 1) * vd_w)
            qr = q_ref[rs, kc]
            kr = k_ref[rs, kc]
            v = v_ref[rs, vc]

            scores = lax.dot_general(qr, kr, (((1,), (1,)), ((), ())),
                                     preferred_element_type=F32) * intra_ref[hh]
            inner = jnp.dot(scores.astype(BF16), v, preferred_element_type=F32)

            state = state_ref[hh]
            qc = (qr.astype(F32) * cross_ref[hh]).astype(BF16)
            cross = jnp.dot(qc, state.astype(BF16), preferred_element_type=F32)

            kd = (kr.astype(F32) * kv_ref[hh]).astype(BF16)
            state_ref[hh] = state * cd_ref[hh] + lax.dot_general(
                kd, v, (((0,), (0,)), ((), ())), preferred_element_type=F32)

            y = inner + cross
            y = y * lax.rsqrt(jnp.mean(y * y, axis=-1, keepdims=True) + EPS)
            o_ref[rs, vc] = (g_ref[rs, vc].astype(F32) * y).astype(o_ref.dtype)


def _retention_tables(seq, key_dim, val_dim, chunk):
    k_scale = key_dim ** -0.5
    inv_freq = ROPE_BASE ** (-np.arange(0, key_dim, 2, dtype=np.float64) / key_dim)
    ang = np.arange(seq, dtype=np.float64)[:, None] * inv_freq[None, :]
    log_gamma = np.log1p(-np.exp2(-5.0 - np.arange(RET_HEADS, dtype=np.float64)))
    idx = np.arange(chunk, dtype=np.float64)
    diff = idx[:, None] - idx[None, :]
    intra = np.where(diff >= 0, np.exp(log_gamma[:, None, None] * np.maximum(diff, 0.0)), 0.0)
    cross = np.exp(log_gamma[:, None] * (idx + 1.0))
    kv = np.exp(log_gamma[:, None] * (chunk - 1.0 - idx))
    cd = np.exp(log_gamma * chunk)
    tables = dict(
        cos=np.cos(ang), sin=np.sin(ang), intra=intra * k_scale,
        cross=np.broadcast_to(cross[:, :, None], (RET_HEADS, chunk, key_dim)),
        kv=np.broadcast_to((kv * k_scale)[:, :, None], (RET_HEADS, chunk, key_dim)),
        cd=np.broadcast_to(cd[:, None, None], (RET_HEADS, 1, val_dim)),
    )
    return {name: jnp.asarray(t, dtype=F32) for name, t in tables.items()}


def _retention(proj, tb, *, batch, seq, d_model):
    chunk = RET_CHUNK
    key_dim = d_model // RET_HEADS
    val_dim = 2 * d_model // RET_HEADS
    proj3 = proj.reshape(batch, seq, 6 * d_model)
    nh = RET_HEADS
    hg = RET_HEAD_GROUP
    ng = nh // hg
    n_sub = RET_CHUNKS_PER_STEP
    blk = chunk * n_sub
    kern = functools.partial(_ret_kernel, chunk=chunk, n_sub=n_sub, heads=hg)
    out = pl.pallas_call(
        kern,
        grid=(batch, ng, seq // blk),
        in_specs=[
            pl.BlockSpec((None, blk, hg * key_dim), lambda b, h, c: (b, c, h)),
            pl.BlockSpec((None, blk, hg * key_dim), lambda b, h, c: (b, c, ng + h)),
            pl.BlockSpec((None, blk, hg * val_dim), lambda b, h, c: (b, c, ng + h)),
            pl.BlockSpec((None, blk, hg * val_dim), lambda b, h, c: (b, c, 2 * ng + h)),
            pl.BlockSpec((hg, chunk, chunk), lambda b, h, c: (h, 0, 0)),
            pl.BlockSpec((hg, chunk, key_dim), lambda b, h, c: (h, 0, 0)),
            pl.BlockSpec((hg, chunk, key_dim), lambda b, h, c: (h, 0, 0)),
            pl.BlockSpec((hg, 1, val_dim), lambda b, h, c: (h, 0, 0)),
        ],
        out_specs=pl.BlockSpec((None, blk, hg * val_dim), lambda b, h, c: (b, c, h)),
        out_shape=jax.ShapeDtypeStruct((batch, seq, nh * val_dim), BF16),
        scratch_shapes=[pltpu.VMEM((hg, key_dim, val_dim), F32)],
        compiler_params=_params("parallel", "parallel", "arbitrary"),
        name="retention",
    )(proj3, proj3, proj3, proj3, tb["intra"], tb["cross"], tb["kv"], tb["cd"])
    return out.reshape(batch * seq, nh * val_dim)


def _tail_kernel(*refs, final_norm, tf, n_cast):
    (h_ref, mix_ref, p_ref, wo_ref, gm_ref, wu_ref, wd_ref, gp_ref, wg_ref, wp_ref,
     fg_ref) = refs[:11]
    cast_in = refs[11:11 + n_cast]
    o_ref = refs[11 + n_cast]
    cast_out = refs[12 + n_cast:]

    h1 = h_ref[...] + jnp.dot(mix_ref[...], wo_ref[...], preferred_element_type=F32)
    xn = _rms_norm(h1, gm_ref[...]).astype(BF16)
    h2 = h1
    for c in range(wu_ref.shape[1] // tf):
        a = jnp.maximum(jnp.dot(xn, wu_ref[:, c * tf:(c + 1) * tf], preferred_element_type=F32), 0.0)
        h2 = h2 + jnp.dot((a * a).astype(BF16), wd_ref[c * tf:(c + 1) * tf, :],
                          preferred_element_type=F32)
    xn = _rms_norm(h2, gp_ref[...]).astype(BF16)
    gate = jax.nn.sigmoid(jnp.dot(xn, wg_ref[...], preferred_element_type=F32))
    up = jnp.dot(p_ref[...].astype(BF16), wp_ref[...], preferred_element_type=F32)
    out = h2 + up * gate
    if final_norm:
        out = _rms_norm(out, fg_ref[...])
    o_ref[...] = out

    for src, dst in zip(cast_in, cast_out):
        dst[...] = src[...].astype(dst.dtype)


def _tail(h, mix, p, layer, w_out, g_mlp, w_up, w_down, g_ple, w_gate, w_ple, final_g, *,
          final_norm, cast=()):
    t, d = h.shape
    dm = mix.shape[1]
    dp = p.shape[2]
    f = w_up.shape[1]
    tm = TAIL_ROW_TILE
    steps = t // tm

    def whole(shape):
        return pl.BlockSpec(shape, lambda i: (0, 0), pipeline_mode=pl.Buffered(1))

    cast_specs, cast_out_specs, cast_shapes = _cast_specs(cast, steps)
    outs = pl.pallas_call(
        functools.partial(_tail_kernel, final_norm=final_norm, tf=MLP_FF_TILE, n_cast=len(cast)),
        grid=(steps,),
        in_specs=[
            pl.BlockSpec((tm, d), lambda i: (i, 0)),
            pl.BlockSpec((tm, dm), lambda i: (i, 0)),
            pl.BlockSpec((None, tm, dp), lambda i: (layer, i, 0)),
            whole((dm, d)),
            whole((1, d)),
            whole((d, f)),
            whole((f, d)),
            whole((1, d)),
            whole((d, d)),
            whole((dp, d)),
            whole((1, d)),
        ] + cast_specs,
        out_specs=[pl.BlockSpec((tm, d), lambda i: (i, 0))] + cast_out_specs,
        out_shape=[jax.ShapeDtypeStruct((t, d), F32)] + cast_shapes,
        compiler_params=_params("parallel"),
        name="tail",
    )(h, mix, p, w_out, g_mlp.reshape(1, d), w_up, w_down, g_ple.reshape(1, d), w_gate, w_ple,
      final_g.reshape(1, d), *[w for w, _ in cast])
    return outs[0], outs[1:]


def kernel(x, p, mix_norm, sb_w_in, sb_w_out, ret_w_in, ret_w_out, mlp_norm, mlp_w_up,
           mlp_w_down, ple_norm, ple_w_gate, ple_w_up, final_norm):
    batch, seq, d_model = x.shape
    depth = p.shape[0]
    t = batch * seq
    h = x.reshape(t, d_model)
    p2 = p.reshape(depth, t, p.shape[-1])

    def layer_weights(i):
        mixer = (sb_w_in, sb_w_out) if i % 2 == 0 else (ret_w_in, ret_w_out)
        return [(mixer[0], i // 2), (mixer[1], i // 2), (mlp_w_up, i), (mlp_w_down, i),
                (ple_w_gate, i)]

    first_w, first_k = layer_weights(0)[0]
    w_in = first_w[first_k].astype(BF16)
    for i in range(depth):
        own = layer_weights(0)[1:] if i == 0 else ()
        if i % 2 == 0:
            q_scale = LOG2E * (d_model // SB_HEADS) ** -0.5
            col_scale = jnp.concatenate([jnp.full((d_model,), q_scale, F32), jnp.ones((2 * d_model,), F32)])
            qkv, cast_own = _norm_proj(h, mix_norm[i], w_in, tn=SB_PROJ_COL_TILE,
                                       col_scale=col_scale, cast=own)
            mix = _sb_attention(qkv, batch=batch, seq=seq, d_model=d_model)
        else:
            tb = _retention_tables(seq, d_model // RET_HEADS, 2 * d_model // RET_HEADS, RET_CHUNK)
            proj, cast_own = _norm_proj(h, mix_norm[i], w_in, tn=RET_PROJ_COL_TILE,
                                        rope=(tb["cos"], tb["sin"], 2 * d_model), seq=seq,
                                        silu_cols=2 * d_model, cast=own)
            mix = _retention(proj, tb, batch=batch, seq=seq, d_model=d_model)
        if own:
            w_out, w_up, w_down, w_gate = cast_own
        last = i == depth - 1
        h, nxt = _tail(h, mix, p2, i, w_out, mlp_norm[i], w_up, w_down, ple_norm[i], w_gate,
                       ple_w_up[i].astype(BF16), final_norm, final_norm=last,
                       cast=() if last else layer_weights(i + 1))
        if not last:
            w_in, w_out, w_up, w_down, w_gate = nxt
    return h.reshape(batch, seq, d_model)
```
